```python
import math
import jax, jax.numpy as jnp
from jax import lax
import numpy as np

D_MODEL = 1024
BATCH = 4
SEQ = 4096
DEPTH = 2
DEC_BATCH = 32
DEC_SEQ = 4
PAST_LEN = 8192
PAGE_SIZE = 128

RWKV_HEAD_DIM = 64
RWKV_HEADS = D_MODEL // 128
RWKV_WIDTH = RWKV_HEADS * RWKV_HEAD_DIM
W_LORA = 64
A_LORA = 64
G_LORA = 128
RWKV_COLS = 3 * RWKV_WIDTH + W_LORA + A_LORA + G_LORA
RWKV_SPLITS = (RWKV_WIDTH, 2 * RWKV_WIDTH, 3 * RWKV_WIDTH,
               3 * RWKV_WIDTH + W_LORA, 3 * RWKV_WIDTH + W_LORA + A_LORA)
DIFF_QK_DIM = 64
DIFF_HEAD_DIM = 2 * DIFF_QK_DIM
DIFF_HEADS = D_MODEL // 256
DIFF_WIDTH = DIFF_HEADS * DIFF_HEAD_DIM
DIFF_COLS = 3 * DIFF_WIDTH
IN_COLS = RWKV_COLS + DIFF_COLS
MIX_WIDTH = RWKV_WIDTH + DIFF_WIDTH
D_FF = 3584
N_EXPERTS = 8
TOP_K = 2
N_DENSE = (DEPTH + 1) // 2
N_MOE = DEPTH // 2
Q_BLOCK = 128
NORM_EPS = 1e-6
SUBLN_EPS = 1e-5
GN_EPS = 64e-5

kernel_name = 'hymba_rwkv7_diffattn_moe_step'


def rms_norm(x, g, eps):
    xf = x.astype(jnp.float32)
    y = xf * lax.rsqrt(jnp.mean(xf * xf, axis=-1, keepdims=True) + eps)
    return (y * g.astype(jnp.float32)).astype(x.dtype)


def alibi_slopes(n_heads):
    return 2.0 ** (-8.0 * jnp.arange(1, n_heads + 1, dtype=jnp.float32) / n_heads)


def wkv_step(S, inp):
    r, w, k, v, a, b = inp
    sa = jnp.einsum('nhij,nhj->nhi', S, a)
    S = S * w[:, :, None, :] + sa[..., None] * b[:, :, None, :] + v[..., None] * k[:, :, None, :]
    return S, jnp.einsum('nhij,nhj->nhi', S, r)


def rwkv7_mix(P, prev_row, S0, mu, w0, w_up, a0, a_up, g_up, k_k, k_a, r_k, lnx_w, lnx_b):
    N, T, _ = P.shape
    P_prev = jnp.concatenate([prev_row[:, None, :].astype(P.dtype), P[:, :-1]], axis=1)
    U = P + mu * (P_prev - P)
    r, k, v, wd, ad, gd = jnp.split(U, RWKV_SPLITS, axis=-1)
    w_log = -jax.nn.softplus(-(w0 + jnp.tanh(wd) @ w_up).astype(jnp.float32)) - 0.5
    decay = jnp.exp(-jnp.exp(w_log))
    a = jax.nn.sigmoid((a0 + ad @ a_up).astype(jnp.float32))
    g = (jax.nn.sigmoid(gd) @ g_up).astype(jnp.float32)

    def heads(z):
        return z.reshape(N, T, RWKV_HEADS, RWKV_HEAD_DIM)

    kf = k.astype(jnp.float32)
    kk = heads(kf * k_k.astype(jnp.float32))
    kk = kk / jnp.maximum(jnp.linalg.norm(kk, axis=-1, keepdims=True), 1e-12)
    k2 = heads(kf * (1.0 + (a - 1.0) * k_a.astype(jnp.float32)))
    rh = heads(r.astype(jnp.float32))
    vh = heads(v.astype(jnp.float32))
    ah = heads(a)
    xs = tuple(jnp.swapaxes(z, 0, 1) for z in (rh, heads(decay), k2, vh, -kk, kk * ah))
    S_final, y = lax.scan(wkv_step, S0.astype(jnp.float32), xs)
    y = jnp.swapaxes(y, 0, 1)
    mean = jnp.mean(y, axis=-1, keepdims=True)
    var = jnp.mean(jnp.square(y - mean), axis=-1, keepdims=True)
    y = ((y - mean) * lax.rsqrt(var + GN_EPS)).reshape(N, T, RWKV_WIDTH)
    y = y * lnx_w.astype(jnp.float32) + lnx_b.astype(jnp.float32)
    bonus = jnp.sum(rh * k2 * r_k.astype(jnp.float32), axis=-1, keepdims=True) * vh
    out = (y + bonus.reshape(N, T, RWKV_WIDTH)) * g
    return out.astype(P.dtype), S_final, P[:, -1]


def diff_qkv(P):
    N, T, _ = P.shape
    q, k, v = jnp.split(P, (DIFF_WIDTH, 2 * DIFF_WIDTH), axis=-1)
    return tuple(z.reshape(N, T, DIFF_HEADS, DIFF_HEAD_DIM) for z in (q, k, v))


def diff_attend(q, k, v, q_pos, k_pos, lam, slopes):
    qf = q.astype(jnp.float32)
    kf = k.astype(jnp.float32)
    dist = (q_pos[:, None] - k_pos[None, :]).astype(jnp.float32)
    bias = -slopes[:, None, None] * dist
    causal = dist >= 0
    scale = DIFF_QK_DIM ** -0.5

    def probs(qa, ka):
        s = jnp.einsum('nqhd,nkhd->nhqk', qa, ka) * scale + bias
        return jax.nn.softmax(jnp.where(causal, s, -jnp.inf), axis=-1)

    attn = probs(qf[..., :DIFF_QK_DIM], kf[..., :DIFF_QK_DIM]) - lam * probs(qf[..., DIFF_QK_DIM:], kf[..., DIFF_QK_DIM:])
    return jnp.einsum('nhqk,nkhd->nqhd', attn, v.astype(jnp.float32))


def diff_attend_blocked(q, k, v, lam, slopes):
    N, T, H, Dh = q.shape
    nb = T // Q_BLOCK
    pos = jnp.arange(T)
    qb = jnp.swapaxes(q.reshape(N, nb, Q_BLOCK, H, Dh), 0, 1)
    qpos = pos.reshape(nb, Q_BLOCK)
    out = lax.map(lambda a: diff_attend(a[0], k, v, a[1], pos, lam, slopes), (qb, qpos))
    return jnp.swapaxes(out, 0, 1).reshape(N, T, H, Dh)


def diff_head_norm(o, w, lam_init, dtype):
    N, T = o.shape[:2]
    o = o * lax.rsqrt(jnp.mean(o * o, axis=-1, keepdims=True) + SUBLN_EPS)
    o = o * w.astype(jnp.float32) * (1.0 - lam_init)
    return o.reshape(N, T, DIFF_WIDTH).astype(dtype)


def swiglu(h, wg, wu, wd):
    return (jax.nn.silu(h @ wg) * (h @ wu)) @ wd


def moe_swiglu(h, router_w, wg, wu, wd):
    logits = (h @ router_w).astype(jnp.float32)
    top_v, top_i = lax.top_k(logits, TOP_K)
    top_g = jax.nn.softmax(top_v, axis=-1)
    gates = jnp.sum(jax.nn.one_hot(top_i, N_EXPERTS, dtype=jnp.float32) * top_g[..., None], axis=-2)
    gates = gates.astype(h.dtype)
    out = jnp.zeros(h.shape[:-1] + (wd.shape[-1],), h.dtype)
    for e in range(N_EXPERTS):
        out = out + gates[..., e:e + 1] * swiglu(h, wg[e], wu[e], wd[e])
    return out


def channel_mix(h, l, dense_w_gate, dense_w_up, dense_w_down, router_w, moe_w_gate, moe_w_up, moe_w_down):
    if l % 2 == 0:
        i = l // 2
        return swiglu(h, dense_w_gate[i], dense_w_up[i], dense_w_down[i])
    i = l // 2
    return moe_swiglu(h, router_w[i], moe_w_gate[i], moe_w_up[i], moe_w_down[i])


def setup_inputs(seed: int = 0) -> dict:
    key = jax.random.key(seed)
    ks = list(jax.random.split(key, 40))
    f32 = jnp.float32

    def nrm(shape, scale):
        return scale * jax.random.normal(ks.pop(), shape, f32)

    def gain(shape):
        return 1.0 + nrm(shape, 0.02)

    n_pages = PAST_LEN // PAGE_SIZE
    n_used = DEC_BATCH * n_pages
    n_pool = n_used + max(1, n_used // 4)
    x_prompt = nrm((BATCH, SEQ, D_MODEL), 1.0)
    x_sample = nrm((DEC_BATCH, DEC_SEQ, D_MODEL), 1.0)
    cache_k = nrm((DEPTH, n_pool, PAGE_SIZE, DIFF_HEADS, DIFF_HEAD_DIM), 1.0)
    cache_v = nrm((DEPTH, n_pool, PAGE_SIZE, DIFF_HEADS, DIFF_HEAD_DIM), 1.0)
    state_wkv = nrm((DEPTH, DEC_BATCH, RWKV_HEADS, RWKV_HEAD_DIM, RWKV_HEAD_DIM), 0.1)
    state_shift = nrm((DEPTH, DEC_BATCH, RWKV_COLS), 1.0)
    page_table = jax.random.permutation(ks.pop(), n_pool)[:n_used].reshape(DEC_BATCH, n_pages).astype(jnp.int32)
    return {
        'x_prompt': x_prompt,
        'x_sample': x_sample,
        'cache_k': cache_k,
        'cache_v': cache_v,
        'state_wkv': state_wkv,
        'state_shift': state_shift,
        'page_table': page_table,
        'norm_mix': gain((DEPTH, D_MODEL)),
        'w_in': nrm((DEPTH, D_MODEL, IN_COLS), D_MODEL ** -0.5),
        'shift_mu': jax.random.uniform(ks.pop(), (DEPTH, RWKV_COLS), f32, 0.0, 1.0),
        'w0': jax.random.uniform(ks.pop(), (DEPTH, RWKV_WIDTH), f32, -6.0, 1.0),
        'w_up': nrm((DEPTH, W_LORA, RWKV_WIDTH), W_LORA ** -0.5),
        'a0': nrm((DEPTH, RWKV_WIDTH), 0.5),
        'a_up': nrm((DEPTH, A_LORA, RWKV_WIDTH), A_LORA ** -0.5),
        'g_up': nrm((DEPTH, G_LORA, RWKV_WIDTH), G_LORA ** -0.5),
        'k_k': 0.85 + nrm((DEPTH, RWKV_WIDTH), 0.05),
        'k_a': 1.0 + nrm((DEPTH, RWKV_WIDTH), 0.05),
        'r_k': nrm((DEPTH, RWKV_HEADS, RWKV_HEAD_DIM), 0.1),
        'lnx_w': gain((DEPTH, RWKV_WIDTH)),
        'lnx_b': nrm((DEPTH, RWKV_WIDTH), 0.02),
        'lam_q1': nrm((DEPTH, DIFF_QK_DIM), 0.1),
        'lam_k1': nrm((DEPTH, DIFF_QK_DIM), 0.1),
        'lam_q2': nrm((DEPTH, DIFF_QK_DIM), 0.1),
        'lam_k2': nrm((DEPTH, DIFF_QK_DIM), 0.1),
        'subln_w': gain((DEPTH, DIFF_HEAD_DIM)),
        'w_out': nrm((DEPTH, MIX_WIDTH, D_MODEL), MIX_WIDTH ** -0.5),
        'norm_ffn': gain((DEPTH, D_MODEL)),
        'dense_w_gate': nrm((N_DENSE, D_MODEL, D_FF), D_MODEL ** -0.5),
        'dense_w_up': nrm((N_DENSE, D_MODEL, D_FF), D_MODEL ** -0.5),
        'dense_w_down': nrm((N_DENSE, D_FF, D_MODEL), D_FF ** -0.5),
        'router_w': nrm((N_MOE, D_MODEL, N_EXPERTS), D_MODEL ** -0.5),
        'moe_w_gate': nrm((N_MOE, N_EXPERTS, D_MODEL, D_FF), D_MODEL ** -0.5),
        'moe_w_up': nrm((N_MOE, N_EXPERTS, D_MODEL, D_FF), D_MODEL ** -0.5),
        'moe_w_down': nrm((N_MOE, N_EXPERTS, D_FF, D_MODEL), D_FF ** -0.5),
        'norm_final': gain((D_MODEL,)),
    }


def reference(x_prompt, x_sample, cache_k, cache_v, state_wkv, state_shift, page_table,
              norm_mix, w_in, shift_mu, w0, w_up, a0, a_up, g_up, k_k, k_a, r_k, lnx_w, lnx_b,
              lam_q1, lam_k1, lam_q2, lam_k2, subln_w, w_out, norm_ffn,
              dense_w_gate, dense_w_up, dense_w_down, router_w, moe_w_gate, moe_w_up, moe_w_down,
              norm_final):
    f32 = jnp.float32
    dtype = x_prompt.dtype
    n_prompt = x_prompt.shape[0]
    n_dec, t_dec = x_sample.shape[:2]
    n_past = page_table.shape[1] * cache_k.shape[2]
    slopes = alibi_slopes(DIFF_HEADS)
    q_pos_s = n_past + jnp.arange(t_dec)
    k_pos_s = jnp.arange(n_past + t_dec)
    xp, xs = x_prompt, x_sample
    pk, pv, sk, sv, pS, pR, sS, sR = [], [], [], [], [], [], [], []
    for l in range(DEPTH):
        lam_init = 0.8 - 0.6 * math.exp(-0.3 * l)
        lam = (jnp.exp(jnp.sum(lam_q1[l].astype(f32) * lam_k1[l].astype(f32)))
               - jnp.exp(jnp.sum(lam_q2[l].astype(f32) * lam_k2[l].astype(f32))) + lam_init)
        rw = (shift_mu[l], w0[l], w_up[l], a0[l], a_up[l], g_up[l], k_k[l], k_a[l], r_k[l], lnx_w[l], lnx_b[l])

        Pp = rms_norm(xp, norm_mix[l], NORM_EPS) @ w_in[l]
        o_rp, S_p, row_p = rwkv7_mix(Pp[..., :RWKV_COLS], jnp.zeros((n_prompt, RWKV_COLS), dtype),
                                     jnp.zeros((n_prompt, RWKV_HEADS, RWKV_HEAD_DIM, RWKV_HEAD_DIM), f32), *rw)
        qp, kp, vp = diff_qkv(Pp[..., RWKV_COLS:])
        o_dp = diff_head_norm(diff_attend_blocked(qp, kp, vp, lam, slopes), subln_w[l], lam_init, dtype)
        xp = xp + jnp.concatenate([o_rp, o_dp], axis=-1) @ w_out[l]
        xp = xp + channel_mix(rms_norm(xp, norm_ffn[l], NORM_EPS), l, dense_w_gate, dense_w_up, dense_w_down,
                              router_w, moe_w_gate, moe_w_up, moe_w_down)

        Ps = rms_norm(xs, norm_mix[l], NORM_EPS) @ w_in[l]
        o_rs, S_s, row_s = rwkv7_mix(Ps[..., :RWKV_COLS], state_shift[l], state_wkv[l], *rw)
        qs, ks_, vs = diff_qkv(Ps[..., RWKV_COLS:])
        past_k = cache_k[l, page_table].reshape(n_dec, n_past, DIFF_HEADS, DIFF_HEAD_DIM)
        past_v = cache_v[l, page_table].reshape(n_dec, n_past, DIFF_HEADS, DIFF_HEAD_DIM)
        k_all = jnp.concatenate([past_k.astype(ks_.dtype), ks_], axis=1)
        v_all = jnp.concatenate([past_v.astype(vs.dtype), vs], axis=1)
        o_ds = diff_head_norm(diff_attend(qs, k_all, v_all, q_pos_s, k_pos_s, lam, slopes), subln_w[l], lam_init, dtype)
        xs = xs + jnp.concatenate([o_rs, o_ds], axis=-1) @ w_out[l]
        xs = xs + channel_mix(rms_norm(xs, norm_ffn[l], NORM_EPS), l, dense_w_gate, dense_w_up, dense_w_down,
                              router_w, moe_w_gate, moe_w_up, moe_w_down)

        pk.append(kp)
        pv.append(vp)
        sk.append(ks_)
        sv.append(vs)
        pS.append(S_p.astype(dtype))
        pR.append(row_p)
        sS.append(S_s.astype(dtype))
        sR.append(row_s)

    y_prompt = rms_norm(xp, norm_final, NORM_EPS)
    y_sample = rms_norm(xs, norm_final, NORM_EPS)
    prompt_k = jnp.stack(pk)
    prompt_v = jnp.stack(pv)
    sample_k = jnp.stack(sk)
    sample_v = jnp.stack(sv)
    prompt_wkv = jnp.stack(pS)
    prompt_shift = jnp.stack(pR)
    sample_wkv = jnp.stack(sS)
    sample_shift = jnp.stack(sR)
    return (y_prompt, y_sample, prompt_k, prompt_v, sample_k, sample_v, prompt_wkv, prompt_shift, sample_wkv, sample_shift)
```

```python
import functools
import math

import jax
import jax.numpy as jnp
from jax import lax
from jax.experimental import pallas as pl
from jax.experimental.pallas import tpu as pltpu

F32 = jnp.float32
BF16 = jnp.bfloat16

RWKV_HEAD_DIM = 64
W_LORA = 64
A_LORA = 64
G_LORA = 128
DIFF_QK_DIM = 64
DIFF_HEAD_DIM = 128
TOP_K = 2
NORM_EPS = 1e-6
SUBLN_EPS = 1e-5
GN_EPS = 64e-5

LANES = 128
CHUNK = 64
PAIR = 2 * RWKV_HEAD_DIM
VMEM_LIMIT = 56 * 1024 * 1024


def _cp(sem, vmem=VMEM_LIMIT):
    return pltpu.CompilerParams(dimension_semantics=sem, vmem_limit_bytes=vmem)


def _dot(a, b):
    return jnp.dot(a, b, preferred_element_type=F32)


def _dot_nt(a, b):
    return lax.dot_general(a, b, (((1,), (1,)), ((), ())), preferred_element_type=F32)


def _dot_tn(a, b):
    return lax.dot_general(a, b, (((0,), (0,)), ((), ())), preferred_element_type=F32)


def _split2(x):
    hi = x.astype(BF16)
    lo = (x - hi.astype(F32)).astype(BF16)
    return hi, lo


def _split3(x):
    h1 = x.astype(BF16)
    r1 = x - h1.astype(F32)
    h2 = r1.astype(BF16)
    h3 = (r1 - h2.astype(F32)).astype(BF16)
    return h1, h2, h3


def _mm1(a, b, dot=_dot):
    return dot(a.astype(BF16), b.astype(BF16))


def _mm3(a, b, dot=_dot):
    ah, al = _split2(a)
    bh, bl = _split2(b)
    return dot(ah, bh) + dot(al, bh) + dot(ah, bl)


def _mm_exact_rhs(a, b_bf16, dot=_dot):
    ah, al = _split2(a)
    return dot(ah, b_bf16) + dot(al, b_bf16)


def _sigmoid(x):
    return 1.0 / (1.0 + jnp.exp(-x))


def _rms(x, g):
    ms = jnp.mean(x * x, axis=-1, keepdims=True)
    return x * lax.rsqrt(ms + NORM_EPS) * g


def _in_proj_kernel(x_ref, g_ref, w_ref, pr_ref, q1_ref, q2_ref, k_ref, v_ref, kb_ref, vb_ref, *,
                    rcols, dw):
    h = _rms(x_ref[...], g_ref[...]).astype(BF16)
    pr_ref[...] = _dot(h, w_ref[:, 0:rcols])
    q = _dot(h, w_ref[:, rcols:rcols + dw]) * (DIFF_QK_DIM ** -0.5)
    lane = lax.broadcasted_iota(jnp.int32, (1, dw), 1) % DIFF_HEAD_DIM
    first = lane < DIFF_QK_DIM
    q1_ref[...] = jnp.where(first, q, 0.0).astype(BF16)
    q2_ref[...] = jnp.where(first, 0.0, q).astype(BF16)
    k = _dot(h, w_ref[:, rcols + dw:rcols + 2 * dw])
    v = _dot(h, w_ref[:, rcols + 2 * dw:rcols + 3 * dw])
    k_ref[...] = k
    v_ref[...] = v
    kb_ref[...] = k.astype(BF16)
    vb_ref[...] = v.astype(BF16)


def _in_proj(x2d, g, w_bf, rcols, dw, tm):
    m, d = x2d.shape
    n_in = w_bf.shape[1]
    row = lambda i: (i, 0)
    const = lambda i: (0, 0)
    outs = (
        jax.ShapeDtypeStruct((m, rcols), F32),
        jax.ShapeDtypeStruct((m, dw), BF16),
        jax.ShapeDtypeStruct((m, dw), BF16),
        jax.ShapeDtypeStruct((m, dw), F32),
        jax.ShapeDtypeStruct((m, dw), F32),
        jax.ShapeDtypeStruct((m, dw), BF16),
        jax.ShapeDtypeStruct((m, dw), BF16),
    )
    return pl.pallas_call(
        functools.partial(_in_proj_kernel, rcols=rcols, dw=dw),
        grid=(m // tm,),
        in_specs=[pl.BlockSpec((tm, d), row), pl.BlockSpec((1, d), const),
                  pl.BlockSpec((d, n_in), const)],
        out_specs=[pl.BlockSpec((tm, rcols), row)] + [pl.BlockSpec((tm, dw), row)] * 6,
        out_shape=outs,
        compiler_params=_cp(("parallel",)),
        name="in_proj",
    )(x2d, g, w_bf)


def _pair_chunk(at, rt, bt, kt, v, pc):
    c = at.shape[0]
    lane = lax.broadcasted_iota(jnp.int32, (1, PAIR), 1)
    first = lane < RWKV_HEAD_DIM

    def stack(x):
        return jnp.concatenate([jnp.where(first, x, 0.0), jnp.where(first, 0.0, x)], axis=0)

    a_s, r_s, b_s, k_s, v_s = stack(at), stack(rt), stack(bt), stack(kt), stack(v)
    g = _mm1(jnp.concatenate([a_s, r_s], axis=0), jnp.concatenate([b_s, k_s], axis=0), _dot_nt)
    ri = lax.broadcasted_iota(jnp.int32, (2 * c, 2 * c), 0) % c
    ci = lax.broadcasted_iota(jnp.int32, (2 * c, 2 * c), 1) % c
    strict = ri > ci
    incl = ri >= ci
    a_ab = jnp.where(strict, g[:2 * c, :2 * c], 0.0)
    a_ak = jnp.where(strict, g[:2 * c, 2 * c:], 0.0)
    a_rb = jnp.where(incl, g[2 * c:, :2 * c], 0.0)
    a_rk = jnp.where(incl, g[2 * c:, 2 * c:], 0.0)
    eye = (lax.broadcasted_iota(jnp.int32, (2 * c, 2 * c), 0)
           == lax.broadcasted_iota(jnp.int32, (2 * c, 2 * c), 1))
    tm = jnp.where(eye, 1.0, 0.0) + a_ab
    pw = a_ab
    span = 2
    while span < c:
        pw = _mm3(pw, pw)
        tm = tm + _mm3(tm, pw)
        span *= 2
    x = jnp.concatenate([a_s, _mm1(a_ak, v_s)], axis=1)
    au = _mm3(tm, x)
    qy = _mm1(a_rb, au)
    q_s = r_s + qy[:, :PAIR]
    y0_s = qy[:, PAIR:] + _mm1(a_rk, v_s)
    q = q_s[:c] + q_s[c:]
    y0 = y0_s[:c] + y0_s[c:]
    bh = b_s * pc
    kh = k_s * pc
    mt = jnp.where(eye, pc, 0.0) + _mm1(bh, au[:, :PAIR], _dot_tn)
    nt = _mm1(bh, au[:, PAIR:], _dot_tn) + _mm1(kh, v_s, _dot_tn)
    return q, y0, mt, nt


def _rwkv_a_kernel(pr_ref, prev8_ref, prev0_ref, mu_ref, lwh_ref, lwl_ref, vec_ref, gup_ref, bd_ref,
                   q_ref, y0_ref, mt_ref, nt_ref, g_ref, bg_ref, *, tb, width, t_valid):
    j = pl.program_id(1)
    p_all = pr_ref[0]
    prev_first = jnp.where(j == 0, prev0_ref[0], prev8_ref[0][7:8, :])
    row = lax.broadcasted_iota(jnp.int32, (tb, 1), 0)
    p_prev = jnp.where(row == 0, prev_first, pltpu.roll(p_all, 1, axis=0))
    u = p_all + mu_ref[...] * (p_prev - p_all)
    r = u[:, 0:width]
    k = u[:, width:2 * width]
    v = u[:, 2 * width:3 * width]
    wa = u[:, 3 * width:3 * width + W_LORA + A_LORA]
    gd = u[:, 3 * width + W_LORA + A_LORA:]
    lane = lax.broadcasted_iota(jnp.int32, (1, W_LORA + A_LORA), 1)
    twa = jnp.where(lane < W_LORA, jnp.tanh(wa), wa)
    th, tl = _split2(twa)
    lora = _dot(th, lwh_ref[...]) + _dot(tl, lwh_ref[...]) + _dot(th, lwl_ref[...])
    w0, a0, k_k, k_a, r_k = (vec_ref[i:i + 1, :] for i in range(5))
    lw = (-math.exp(-0.5)) * _sigmoid(w0 + lora[:, :width])
    alpha = _sigmoid(a0 + lora[:, width:])
    g = _dot(_sigmoid(gd).astype(BF16), gup_ref[...])
    bd = bd_ref[...]
    kk = k * k_k
    kkn = kk / jnp.maximum(jnp.sqrt(_mm_exact_rhs(kk * kk, bd)), 1e-12)
    k2 = k * (1.0 + (alpha - 1.0) * k_a)
    bonus = _mm_exact_rhs(r * k2 * r_k, bd) * v
    if t_valid is not None:
        live = (j * tb + row) < t_valid
        lw = jnp.where(live, lw, 0.0)
        kkn = jnp.where(live, kkn, 0.0)
        k2 = jnp.where(live, k2, 0.0)
        v = jnp.where(live, v, 0.0)
    ri = lax.broadcasted_iota(jnp.int32, (tb, tb), 0)
    ci = lax.broadcasted_iota(jnp.int32, (tb, tb), 1)
    tri = jnp.where((ri // CHUNK == ci // CHUNK) & (ci <= ri), 1.0, 0.0).astype(BF16)
    l1, l2, l3 = _split3(lw)
    cum = _dot(tri, l1) + _dot(tri, l2) + _dot(tri, l3)
    e_in = jnp.exp(cum)
    e_ex = jnp.exp(cum - lw)
    e_neg = jnp.exp(-cum)
    rt = r * e_in
    at = -kkn * e_ex
    bt = kkn * alpha * e_neg
    kt = k2 * e_neg
    g_ref[0] = g
    bg_ref[0] = bonus * g
    for c in range(tb // CHUNK):
        rows = slice(c * CHUNK, (c + 1) * CHUNK)
        pc = e_in[(c + 1) * CHUNK - 1:(c + 1) * CHUNK, :]
        for p in range(width // PAIR):
            cols = slice(p * PAIR, (p + 1) * PAIR)
            q, y0, mt, nt = _pair_chunk(at[rows, cols], rt[rows, cols], bt[rows, cols],
                                        kt[rows, cols], v[rows, cols], pc[:, cols])
            q_ref[0, rows, cols] = q
            y0_ref[0, rows, cols] = y0
            mt_ref[0, c, p] = mt
            nt_ref[0, c, p] = nt


def _rwkv_b_kernel(q_ref, y0_ref, mt_ref, nt_ref, g_ref, bg_ref, z0_ref, ln_ref, bd_ref,
                   o_ref, zout_ref, z_scr, y_scr, *, tb, width):
    j = pl.program_id(1)

    @pl.when(j == 0)
    def _():
        z_scr[...] = z0_ref[0]

    for c in range(tb // CHUNK):
        rows = slice(c * CHUNK, (c + 1) * CHUNK)
        for p in range(width // PAIR):
            cols = slice(p * PAIR, (p + 1) * PAIR)
            z = z_scr[p]
            y_scr[rows, cols] = _mm3(q_ref[0, rows, cols], z) + y0_ref[0, rows, cols]
            z_scr[p] = _mm3(mt_ref[0, c, p], z) + nt_ref[0, c, p]
    y = y_scr[...]
    bd = bd_ref[...]
    inv = 1.0 / RWKV_HEAD_DIM
    mean = _mm_exact_rhs(y, bd) * inv
    d = y - mean
    var = _mm_exact_rhs(d * d, bd) * inv
    yn = d * lax.rsqrt(var + GN_EPS) * ln_ref[0:1, :] + ln_ref[1:2, :]
    o_ref[0] = (yn * g_ref[0] + bg_ref[0]).astype(BF16)

    @pl.when(j == pl.num_programs(1) - 1)
    def _():
        zout_ref[0] = z_scr[...]


def _rwkv_mix(pr, prev_row, z0, rw, t_valid, tb_a, tb):
    ns, t, rcols = pr.shape
    width = rw["bd"].shape[0]
    n_pair = width // PAIR
    c3 = lambda n, j: (n, j, 0)
    const2 = lambda n, j: (0, 0)
    tok_shape = jax.ShapeDtypeStruct((ns, t, width), F32)
    mat_shape = jax.ShapeDtypeStruct((ns, t // CHUNK, n_pair, PAIR, PAIR), F32)
    tok = pl.BlockSpec((1, tb_a, width), c3)
    mat = pl.BlockSpec((1, tb_a // CHUNK, n_pair, PAIR, PAIR), lambda n, j: (n, j, 0, 0, 0))
    q, y0, mt, nt, g, bg = pl.pallas_call(
        functools.partial(_rwkv_a_kernel, tb=tb_a, width=width, t_valid=t_valid),
        grid=(ns, t // tb_a),
        in_specs=[
            pl.BlockSpec((1, tb_a, rcols), c3),
            pl.BlockSpec((1, 8, rcols), lambda n, j: (n, jnp.maximum(j * (tb_a // 8) - 1, 0), 0)),
            pl.BlockSpec((1, 1, rcols), lambda n, j: (n, 0, 0)),
            pl.BlockSpec((1, rcols), const2),
            pl.BlockSpec(rw["lora_hi"].shape, const2),
            pl.BlockSpec(rw["lora_lo"].shape, const2),
            pl.BlockSpec(rw["vec"].shape, const2),
            pl.BlockSpec(rw["g_up"].shape, const2),
            pl.BlockSpec((width, width), const2),
        ],
        out_specs=[tok, tok, mat, mat, tok, tok],
        out_shape=(tok_shape, tok_shape, mat_shape, mat_shape, tok_shape, tok_shape),
        compiler_params=_cp(("parallel", "parallel")),
        name="rwkv_chunk_summaries",
    )(pr, pr, prev_row, rw["mu"], rw["lora_hi"], rw["lora_lo"], rw["vec"], rw["g_up"], rw["bd"])
    zspec = pl.BlockSpec((1, n_pair, PAIR, PAIR), lambda n, j: (n, 0, 0, 0))
    tok = pl.BlockSpec((1, tb, width), c3)
    mat = pl.BlockSpec((1, tb // CHUNK, n_pair, PAIR, PAIR), lambda n, j: (n, j, 0, 0, 0))
    o, z_end = pl.pallas_call(
        functools.partial(_rwkv_b_kernel, tb=tb, width=width),
        grid=(ns, t // tb),
        in_specs=[tok, tok, mat, mat, tok, tok, zspec,
                  pl.BlockSpec((2, width), const2), pl.BlockSpec((width, width), const2)],
        out_specs=[tok, zspec],
        out_shape=(jax.ShapeDtypeStruct((ns, t, width), BF16),
                   jax.ShapeDtypeStruct((ns, n_pair, PAIR, PAIR), F32)),
        scratch_shapes=[pltpu.VMEM((n_pair, PAIR, PAIR), F32), pltpu.VMEM((tb, width), F32)],
        compiler_params=_cp(("parallel", "arbitrary")),
        name="rwkv_state_pass",
    )(q, y0, mt, nt, g, bg, z0, rw["ln"], rw["bd"])
    return o, z_end


def _state_to_z(s):
    ns, nh, dv, dk = s.shape
    st = jnp.swapaxes(s, -1, -2).reshape(ns, nh // 2, 2, dk, dv)
    z = jnp.zeros((ns, nh // 2, 2, dk, 2, dv), s.dtype)
    z = z.at[:, :, 0, :, 0, :].set(st[:, :, 0]).at[:, :, 1, :, 1, :].set(st[:, :, 1])
    return z.reshape(ns, nh // 2, 2 * dk, 2 * dv)


def _z_to_state(z):
    ns, npair = z.shape[:2]
    d = RWKV_HEAD_DIM
    z6 = z.reshape(ns, npair, 2, d, 2, d)
    st = jnp.stack([z6[:, :, 0, :, 0, :], z6[:, :, 1, :, 1, :]], axis=2)
    return jnp.swapaxes(st, -1, -2).reshape(ns, npair * 2, d, d)


def _diff_flash_kernel(lam_ref, q1_ref, q2_ref, k_ref, v_ref, w_ref, o_ref,
                       m1, l1, a1, m2, l2, a2, *, tq, tk, lam_init):
    h = pl.program_id(1)
    qi = pl.program_id(2)
    kj = pl.program_id(3)

    @pl.when(kj == 0)
    def _():
        for m, l, a in ((m1, l1, a1), (m2, l2, a2)):
            m[...] = jnp.full(m.shape, -jnp.inf, F32)
            l[...] = jnp.zeros(l.shape, F32)
            a[...] = jnp.zeros(a.shape, F32)

    @pl.when(kj <= qi)
    def _():
        k = k_ref[...]
        v = v_ref[...]
        slope = lam_ref[1 + h]
        dist =((qi * tq + lax.broadcasted_iota(jnp.int32, (tq, tk), 0))
                - (kj * tk + lax.broadcasted_iota(jnp.int32, (tq, tk), 1)))
        bias = jnp.where(dist >= 0, -slope * dist.astype(F32), -jnp.inf)
        for q_ref, m, l, a in ((q1_ref, m1, l1, a1), (q2_ref, m2, l2, a2)):
            s = _dot_nt(q_ref[...], k) + bias
            m_old = m[...]
            m_new = jnp.maximum(m_old, jnp.max(s, axis=-1, keepdims=True))
            p = jnp.exp(s - m_new)
            scale = jnp.exp(m_old - m_new)
            l[...] = scale * l[...] + jnp.sum(p, axis=-1, keepdims=True)
            a[...] = scale * a[...] + _dot(p.astype(BF16), v)
            m[...] = m_new

    @pl.when(kj == pl.num_programs(3) - 1)
    def _():
        o = a1[...] / l1[...] - lam_ref[0] * (a2[...] / l2[...])
        o = o * lax.rsqrt(jnp.mean(o * o, axis=-1, keepdims=True) + SUBLN_EPS)
        o_ref[...] = (o * w_ref[...] * (1.0 - lam_init)).astype(BF16)


def _diff_attn_prompt(lam, q1, q2, kb, vb, subln_w, ns, t, lam_init, tq):
    m, dw = q1.shape
    nh = dw // DIFF_HEAD_DIM
    nq = t // tq
    qmap = lambda n, h, i, j: (n * nq + i, h)
    kmap = lambda n, h, i, j: (n * nq + jnp.minimum(j, i), h)
    blk = (tq, DIFF_HEAD_DIM)
    return pl.pallas_call(
        functools.partial(_diff_flash_kernel, tq=tq, tk=tq, lam_init=lam_init),
        grid=(ns, nh, nq, nq),
        in_specs=[pl.BlockSpec(memory_space=pltpu.SMEM),
                  pl.BlockSpec(blk, qmap), pl.BlockSpec(blk, qmap),
                  pl.BlockSpec(blk, kmap), pl.BlockSpec(blk, kmap),
                  pl.BlockSpec((1, DIFF_HEAD_DIM), lambda n, h, i, j: (0, 0))],
        out_specs=pl.BlockSpec(blk, qmap),
        out_shape=jax.ShapeDtypeStruct((m, dw), BF16),
        scratch_shapes=[pltpu.VMEM((tq, 1), F32), pltpu.VMEM((tq, 1), F32), pltpu.VMEM(blk, F32),
                        pltpu.VMEM((tq, 1), F32), pltpu.VMEM((tq, 1), F32), pltpu.VMEM(blk, F32)],
        compiler_params=_cp(("parallel", "parallel", "parallel", "arbitrary")),
        name="diff_attn_prompt",
    )(lam, q1, q2, kb, vb, subln_w)


def _diff_decode_kernel(pt_ref, lam_ref, q_ref, *refs, n_grp, page, n_past, t_dec, t_pad, nh, lam_init):
    k_refs = refs[:n_grp]
    v_refs = refs[n_grp:2 * n_grp]
    kn_ref, vn_ref, w_ref, sl_ref, o_ref, m_scr, l_scr, acc = refs[2 * n_grp:]
    g = pl.program_id(1)
    rows = 2 * nh * t_pad
    dw = nh * DIFF_HEAD_DIM

    @pl.when(g == 0)
    def _():
        m_scr[...] = jnp.full(m_scr.shape, -jnp.inf, F32)
        l_scr[...] = jnp.zeros(l_scr.shape, F32)
        acc[...] = jnp.zeros(acc.shape, F32)

    q = q_ref[0]
    row = lax.broadcasted_iota(jnp.int32, (rows, 1), 0)
    trow = row % t_pad
    slope = sl_ref[...]

    def update(s, vals):
        m_old = m_scr[...]
        m_new = jnp.maximum(m_old, jnp.max(s, axis=-1, keepdims=True))
        p = jnp.exp(s - m_new)
        scale = jnp.exp(m_old - m_new)
        l_scr[...] = scale * l_scr[...] + jnp.sum(p, axis=-1, keepdims=True)
        pv = None
        for i, vmat in enumerate(vals):
            part = _dot(p[:, i * page:(i + 1) * page].astype(BF16), vmat)
            pv = part if pv is None else pv + part
        acc[...] = scale * acc[...] + pv
        m_scr[...] = m_new

    s = jnp.concatenate([_dot_nt(q, k_refs[i][0].astype(BF16)) for i in range(n_grp)], axis=1)
    kpos = g * (n_grp * page) + lax.broadcasted_iota(jnp.int32, (1, n_grp * page), 1)
    s = s - slope * ((n_past + trow) - kpos).astype(F32)
    update(s, [v_refs[i][0].astype(BF16) for i in range(n_grp)])

    @pl.when(g == pl.num_programs(1) - 1)
    def _():
        c = lax.broadcasted_iota(jnp.int32, (1, page), 1)
        ok = (c <= trow) & (c < t_dec)
        sn = _dot_nt(q, kn_ref[0])
        sn = jnp.where(ok, sn - slope * (trow - c).astype(F32), -jnp.inf)
        update(sn, [vn_ref[0]])
        o = acc[...] / l_scr[...]
        half = nh * t_pad
        d = o[:half] - lam_ref[0] * o[half:]
        lane_head = lax.broadcasted_iota(jnp.int32, (1, dw), 1) // DIFF_HEAD_DIM
        out = jnp.zeros((t_pad, dw), F32)
        for hh in range(nh):
            out = out + jnp.where(lane_head == hh, d[hh * t_pad:(hh + 1) * t_pad], 0.0)
        w = w_ref[...]
        for hh in range(nh):
            cols = slice(hh * DIFF_HEAD_DIM, (hh + 1) * DIFF_HEAD_DIM)
            blk = out[:, cols]
            blk = blk * lax.rsqrt(jnp.mean(blk * blk, axis=-1, keepdims=True) + SUBLN_EPS)
            o_ref[0, :, cols] = (blk * w * (1.0 - lam_init)).astype(BF16)


def _diff_attn_decode(page_ids, lam, qbd, cache_k, cache_v, kn, vn, subln_w, row_slope, t_dec, lam_init,
                      n_grp):
    nb, rows, dw = qbd.shape
    page = cache_k.shape[1]
    n_pages = page_ids.shape[0] // nb
    nh = dw // DIFF_HEAD_DIM
    t_pad = rows // (2 * nh)
    ng = n_pages // n_grp

    def page_spec(i):
        return pl.BlockSpec((1, page, dw), lambda b, g, pt: (pt[b * n_pages + g * n_grp + i], 0, 0))

    per_b = lambda b, g, pt: (b, 0, 0)
    grid_spec = pltpu.PrefetchScalarGridSpec(
        num_scalar_prefetch=1,
        grid=(nb, ng),
        in_specs=[pl.BlockSpec(memory_space=pltpu.SMEM), pl.BlockSpec((1, rows, dw), per_b)]
        + [page_spec(i) for i in range(n_grp)] * 2
        + [pl.BlockSpec((1, page, dw), per_b), pl.BlockSpec((1, page, dw), per_b),
           pl.BlockSpec((1, DIFF_HEAD_DIM), lambda b, g, pt: (0, 0)),
           pl.BlockSpec((rows, 1), lambda b, g, pt: (0, 0))],
        out_specs=pl.BlockSpec((1, t_pad, dw), per_b),
        scratch_shapes=[pltpu.VMEM((rows, 1), F32), pltpu.VMEM((rows, 1), F32),
                        pltpu.VMEM((rows, dw), F32)],
    )
    return pl.pallas_call(
        functools.partial(_diff_decode_kernel, n_grp=n_grp, page=page, n_past=n_pages * page,
                          t_dec=t_dec, t_pad=t_pad, nh=nh, lam_init=lam_init),
        grid_spec=grid_spec,
        out_shape=jax.ShapeDtypeStruct((nb, t_pad, dw), BF16),
        compiler_params=_cp(("parallel", "arbitrary")),
        name="diff_attn_decode",
    )(page_ids, lam, qbd, *([cache_k] * n_grp), *([cache_v] * n_grp), kn, vn, subln_w, row_slope)


def _out_proj_kernel(x_ref, or_ref, od_ref, w_ref, g_ref, *rest, width, routed):
    if routed:
        rh_ref, rl_ref, xn_ref, hn_ref, lg_ref = rest
    else:
        xn_ref, hn_ref = rest
    xn = x_ref[...] + _dot(or_ref[...], w_ref[0:width, :]) + _dot(od_ref[...], w_ref[width:, :])
    xn_ref[...] = xn
    hn = _rms(xn, g_ref[...])
    hn_ref[...] = hn.astype(hn_ref.dtype)
    if routed:
        hh, hl = _split2(hn)
        lg_ref[...] = _dot(hh, rh_ref[...]) + _dot(hl, rh_ref[...]) + _dot(hh, rl_ref[...])


def _out_proj(x2d, o_r, o_d, w_bf, g, router, tm):
    m, d = x2d.shape
    width = o_r.shape[1]
    row = lambda i: (i, 0)
    const = lambda i: (0, 0)
    routed = router is not None
    in_specs = [pl.BlockSpec((tm, d), row), pl.BlockSpec((tm, width), row),
                pl.BlockSpec((tm, o_d.shape[1]), row), pl.BlockSpec(w_bf.shape, const),
                pl.BlockSpec((1, d), const)]
    args = [x2d, o_r, o_d, w_bf, g]
    out_specs = [pl.BlockSpec((tm, d), row), pl.BlockSpec((tm, d), row)]
    out_shape = [jax.ShapeDtypeStruct((m, d), F32),
                 jax.ShapeDtypeStruct((m, d), F32 if routed else BF16)]
    if routed:
        in_specs += [pl.BlockSpec(router[0].shape, const)] * 2
        args += list(router)
        out_specs.append(pl.BlockSpec((tm, LANES), row))
        out_shape.append(jax.ShapeDtypeStruct((m, LANES), F32))
    return pl.pallas_call(
        functools.partial(_out_proj_kernel, width=width, routed=routed),
        grid=(m // tm,),
        in_specs=in_specs, out_specs=out_specs, out_shape=out_shape,
        compiler_params=_cp(("parallel",)),
        name="out_proj",
    )(*args)


def _swiglu_partial(h, wg, wu, wd):
    g = _dot(h, wg)
    u = _dot(h, wu)
    return _dot((g * _sigmoid(g) * u).astype(BF16), wd)


def _ffn_kernel(h_ref, x_ref, wg_ref, wu_ref, wd_ref, gf_ref, o_ref, acc, *, final_norm):
    f = pl.program_id(1)
    part = _swiglu_partial(h_ref[...], wg_ref[...], wu_ref[...], wd_ref[...])

    @pl.when(f == 0)
    def _():
        acc[...] = part

    @pl.when(f > 0)
    def _():
        acc[...] += part

    @pl.when(f == pl.num_programs(1) - 1)
    def _():
        y = x_ref[...] + acc[...]
        o_ref[...] = _rms(y, gf_ref[...]) if final_norm else y


def _ffn_dense(hn, x2d, wg, wu, wd, g_final, final_norm, tm, tf):
    m, d = x2d.shape
    dff = wg.shape[1]
    row = lambda i, f: (i, 0)
    return pl.pallas_call(
        functools.partial(_ffn_kernel, final_norm=final_norm),
        grid=(m // tm, dff // tf),
        in_specs=[pl.BlockSpec((tm, d), row), pl.BlockSpec((tm, d), row),
                  pl.BlockSpec((d, tf), lambda i, f: (0, f)), pl.BlockSpec((d, tf), lambda i, f: (0, f)),
                  pl.BlockSpec((tf, d), lambda i, f: (f, 0)), pl.BlockSpec((1, d), lambda i, f: (0, 0))],
        out_specs=pl.BlockSpec((tm, d), row),
        out_shape=jax.ShapeDtypeStruct((m, d), F32),
        scratch_shapes=[pltpu.VMEM((tm, d), F32)],
        compiler_params=_cp(("parallel", "arbitrary")),
        name="ffn_dense",
    )(hn, x2d, wg, wu, wd, g_final)


def _moe_kernel(te_ref, nv_ref, nt_ref, tok_ref, dst_ref, h_hbm, gate_ref, wg_ref, wu_ref, wd_ref,
                o_hbm, xg, xb, acc, sem_in, sem_out, *, tm):
    i = pl.program_id(0)
    f = pl.program_id(1)
    base = i * tm

    @pl.when(i < nt_ref[0])
    def _():
        @pl.when(f == 0)
        def _():
            def issue(r, carry):
                pltpu.make_async_copy(h_hbm.at[pl.ds(tok_ref[base + r], 1)], xg.at[pl.ds(r, 1)],
                                      sem_in).start()
                return carry
            lax.fori_loop(0, tm, issue, 0)
            pltpu.make_async_copy(h_hbm.at[pl.ds(0, tm)], xg, sem_in).wait()
            xb[...] = xg[...].astype(BF16)

        part = _swiglu_partial(xb[...], wg_ref[0], wu_ref[0], wd_ref[0])

        @pl.when(f == 0)
        def _():
            acc[...] = part

        @pl.when(f > 0)
        def _():
            acc[...] += part

        @pl.when(f == pl.num_programs(1) - 1)
        def _():
            n_valid = nv_ref[i]
            xg[...] = acc[...] * gate_ref[...]

            def issue(r, carry):
                pltpu.make_async_copy(xg.at[pl.ds(r, 1)], o_hbm.at[pl.ds(dst_ref[base + r], 1)],
                                      sem_out).start()
                return carry
            lax.fori_loop(0, n_valid, issue, 0)
            size = tm
            while size >= 1:
                @pl.when((n_valid & size) != 0)
                def _(size=size):
                    pltpu.make_async_copy(xg.at[pl.ds(0, size)], o_hbm.at[pl.ds(0, size)], sem_out).wait()
                size //= 2


def _moe_experts(plan, hn, wg, wu, wd, n_rows_out, tm, tf):
    m, d = hn.shape
    dff = wg.shape[2]
    n_tiles = plan["tile_expert"].shape[0]
    nf = dff // tf

    def fblk(i, f, nt):
        return jnp.where(i < nt[0], f, nf - 1)

    wmap_in = lambda i, f, te, nv, nt, tok, dst: (te[i], 0, fblk(i, f, nt))
    wmap_out = lambda i, f, te, nv, nt, tok, dst: (te[i], fblk(i, f, nt), 0)
    grid_spec = pltpu.PrefetchScalarGridSpec(
        num_scalar_prefetch=5,
        grid=(n_tiles, dff // tf),
        in_specs=[pl.BlockSpec(memory_space=pl.ANY),
                  pl.BlockSpec((tm, 1), lambda i, f, te, nv, nt, tok, dst: (i, 0)),
                  pl.BlockSpec((1, d, tf), wmap_in), pl.BlockSpec((1, d, tf), wmap_in),
                  pl.BlockSpec((1, tf, d), wmap_out)],
        out_specs=pl.BlockSpec(memory_space=pl.ANY),
        scratch_shapes=[pltpu.VMEM((tm, d), F32), pltpu.VMEM((tm, d), BF16), pltpu.VMEM((tm, d), F32),
                        pltpu.SemaphoreType.DMA, pltpu.SemaphoreType.DMA],
    )
    return pl.pallas_call(
        functools.partial(_moe_kernel, tm=tm),
        grid_spec=grid_spec,
        out_shape=jax.ShapeDtypeStruct((n_rows_out, d), F32),
        compiler_params=_cp(("arbitrary", "arbitrary")),
        name="moe_experts",
    )(plan["tile_expert"], plan["tile_valid"], plan["n_tiles"], plan["row_token"], plan["row_dst"],
      hn, plan["row_gate"], wg, wu, wd)


def _route(logits, n_experts, tm):
    m = logits.shape[0]
    top_v, top_i = lax.top_k(logits[:, :n_experts], TOP_K)
    gates = jax.nn.softmax(top_v, axis=-1)
    e_flat = top_i.reshape(-1).astype(jnp.int32)
    n_assign = m * TOP_K
    onehot = (e_flat[:, None] == jnp.arange(n_experts, dtype=jnp.int32)[None, :]).astype(jnp.int32)
    rank = jnp.sum((jnp.cumsum(onehot, axis=0) - onehot) * onehot, axis=1)
    counts = jnp.sum(onehot, axis=0)
    tiles_per = (counts + tm - 1) // tm
    tile_end = jnp.cumsum(tiles_per)
    tile_start = tile_end - tiles_per
    pos = tile_start[e_flat] * tm + rank
    n_tiles_max = n_assign // tm + n_experts
    n_rows = n_tiles_max * tm
    a = jnp.arange(n_assign, dtype=jnp.int32)
    token = a // TOP_K
    slot = a % TOP_K
    row_token = jnp.zeros((n_rows,), jnp.int32).at[pos].set(token)
    row_dst = jnp.zeros((n_rows,), jnp.int32).at[pos].set(slot * m + token)
    row_gate = jnp.zeros((n_rows,), F32).at[pos].set(gates.reshape(-1))
    tiles = jnp.arange(n_tiles_max, dtype=jnp.int32)
    used = tiles < tile_end[-1]
    tile_expert = jnp.sum((jnp.minimum(tiles, tile_end[-1] - 1)[:, None] >= tile_end[None, :])
                          .astype(jnp.int32), axis=1)
    filled = jnp.clip(counts[tile_expert] - (tiles - tile_start[tile_expert]) * tm, 0, tm)
    tile_valid = jnp.where(used, filled, 0).astype(jnp.int32)
    return {
        "tile_expert": tile_expert.astype(jnp.int32),
        "tile_valid": tile_valid,
        "n_tiles": tile_end[-1:].astype(jnp.int32),
        "row_token": row_token,
        "row_dst": row_dst,
        "row_gate": row_gate.reshape(n_rows, 1),
    }


def _combine_kernel(x_ref, y0_ref, y1_ref, g_ref, o_ref, *, final_norm):
    y = x_ref[...] + y0_ref[...] + y1_ref[...]
    o_ref[...] = _rms(y, g_ref[...]) if final_norm else y


def _combine(x2d, y2, g_final, final_norm, tm):
    m, d = x2d.shape
    nblk = m // tm
    return pl.pallas_call(
        functools.partial(_combine_kernel, final_norm=final_norm),
        grid=(nblk,),
        in_specs=[pl.BlockSpec((tm, d), lambda i: (i, 0)), pl.BlockSpec((tm, d), lambda i: (i, 0)),
                  pl.BlockSpec((tm, d), lambda i: (i + nblk, 0)), pl.BlockSpec((1, d), lambda i: (0, 0))],
        out_specs=pl.BlockSpec((tm, d), lambda i: (i, 0)),
        out_shape=jax.ShapeDtypeStruct((m, d), F32),
        compiler_params=_cp(("parallel",)),
        name="moe_combine",
    )(x2d, y2, y2, g_final)


def _tile(m, pref):
    t = min(m, pref)
    while m % t:
        t //= 2
    return t


def _block_diag_ones(width, blk):
    i = jnp.arange(width) // blk
    return (i[:, None] == i[None, :]).astype(BF16)


def _rwkv_params(l, shift_mu, w0, w_up, a0, a_up, g_up, k_k, k_a, r_k, lnx_w, lnx_b):
    width = w0.shape[1]
    lora = jnp.zeros((W_LORA + A_LORA, 2 * width), F32)
    lora = lora.at[:W_LORA, :width].set(w_up[l]).at[W_LORA:, width:].set(a_up[l])
    lora_hi = lora.astype(BF16)
    lora_lo = (lora - lora_hi.astype(F32)).astype(BF16)
    vec = jnp.zeros((8, width), F32)
    vec = vec.at[0].set(w0[l]).at[1].set(a0[l]).at[2].set(k_k[l]).at[3].set(k_a[l])
    vec = vec.at[4].set(r_k[l].reshape(-1))
    return {
        "mu": shift_mu[l][None, :], "lora_hi": lora_hi, "lora_lo": lora_lo, "vec": vec,
        "g_up": g_up[l].astype(BF16), "bd": _block_diag_ones(width, RWKV_HEAD_DIM),
        "ln": jnp.stack([lnx_w[l], lnx_b[l]]),
    }


def kernel(x_prompt, x_sample, cache_k, cache_v, state_wkv, state_shift, page_table, norm_mix, w_in, shift_mu, w0, w_up, a0, a_up, g_up, k_k, k_a, r_k, lnx_w, lnx_b, lam_q1, lam_k1, lam_q2, lam_k2, subln_w, w_out, norm_ffn, dense_w_gate, dense_w_up, dense_w_down, router_w, moe_w_gate, moe_w_up, moe_w_down, norm_final):
    depth = w_in.shape[0]
    nb_p, t_p, d = x_prompt.shape
    nb_s, t_s, _ = x_sample.shape
    width = w0.shape[1]
    rcols = shift_mu.shape[1]
    dw = (w_in.shape[2] - rcols) // 3
    nh_d = dw // DIFF_HEAD_DIM
    nh_r = width // RWKV_HEAD_DIM
    n_pool, page = cache_k.shape[1], cache_k.shape[2]
    n_pages = page_table.shape[1]
    n_experts = router_w.shape[2]
    m_p = nb_p * t_p
    m_s = nb_s * t_s

    xp = x_prompt.reshape(m_p, d)
    xs = x_sample.reshape(m_s, d)
    ck = cache_k.reshape(depth * n_pool, page, dw)
    cv = cache_v.reshape(depth * n_pool, page, dw)
    t_pad = 8
    head_mask = (jnp.arange(dw)[None, :] // DIFF_HEAD_DIM == jnp.arange(nh_d)[:, None]).astype(BF16)
    slopes = 2.0 ** (-8.0 * jnp.arange(1, nh_d + 1, dtype=F32) / nh_d)
    row_slope = jnp.tile(jnp.repeat(slopes, t_pad), 2)[:, None]

    tm_p = _tile(m_p, 512)
    tm_s = _tile(m_s, 128)
    pk, pv, sk, sv, p_state, p_shift, s_state, s_shift = [], [], [], [], [], [], [], []

    for l in range(depth):
        lam_init = 0.8 - 0.6 * math.exp(-0.3 * l)
        lam = (jnp.exp(jnp.sum(lam_q1[l] * lam_k1[l])) - jnp.exp(jnp.sum(lam_q2[l] * lam_k2[l]))
               + lam_init).reshape(1).astype(F32)
        lam = jnp.concatenate([lam, slopes])
        rw = _rwkv_params(l, shift_mu, w0, w_up, a0, a_up, g_up, k_k, k_a, r_k, lnx_w, lnx_b)
        w_in_bf = w_in[l].astype(BF16)
        w_out_bf = w_out[l].astype(BF16)
        g_mix = norm_mix[l][None, :]
        g_ffn = norm_ffn[l][None, :]
        g_fin = norm_final[None, :]
        sub_w = subln_w[l][None, :]
        last = l == depth - 1
        routed = l % 2 == 1
        router = None
        if routed:
            rt = jnp.zeros((d, LANES), F32).at[:, :n_experts].set(router_w[l // 2])
            rt_hi = rt.astype(BF16)
            router = (rt_hi, (rt - rt_hi.astype(F32)).astype(BF16))

        pr, q1, q2, k_f, v_f, k_b, v_b = _in_proj(xp, g_mix, w_in_bf, rcols, dw, tm_p)
        pr3 = pr.reshape(nb_p, t_p, rcols)
        z0 = jnp.zeros((nb_p, nh_r // 2, PAIR, PAIR), F32)
        o_r, z_end = _rwkv_mix(pr3, jnp.zeros((nb_p, 1, rcols), F32), z0, rw, None,
                               _tile(t_p, 128), _tile(t_p, 256))
        o_d = _diff_attn_prompt(lam, q1, q2, k_b, v_b, sub_w, nb_p, t_p, lam_init, _tile(t_p, 512))
        pk.append(k_f.reshape(nb_p, t_p, nh_d, DIFF_HEAD_DIM))
        pv.append(v_f.reshape(nb_p, t_p, nh_d, DIFF_HEAD_DIM))
        p_state.append(_z_to_state(z_end))
        p_shift.append(pr3[:, -1])
        xp_res = _out_proj(xp, o_r.reshape(m_p, width), o_d, w_out_bf, g_ffn, router, tm_p)

        pr_s, q1s, q2s, ks_f, vs_f, ks_b, vs_b = _in_proj(xs, g_mix, w_in_bf, rcols, dw, tm_s)
        pr_s3 = pr_s.reshape(nb_s, t_s, rcols)
        pr_pad = jnp.pad(pr_s3, ((0, 0), (0, CHUNK - t_s), (0, 0)))
        o_rs, z_end_s = _rwkv_mix(pr_pad, state_shift[l][:, None, :], _state_to_z(state_wkv[l]), rw,
                                  t_s, CHUNK, CHUNK)
        o_rs = o_rs[:, :t_s].reshape(m_s, width)
        qs = jnp.stack([q1s, q2s]).reshape(2, nb_s, 1, t_s, dw) * head_mask[None, None, :, None, :]
        qs = jnp.pad(qs, ((0, 0), (0, 0), (0, 0), (0, t_pad - t_s), (0, 0)))
        qbd = jnp.moveaxis(qs, 0, 1).reshape(nb_s, 2 * nh_d * t_pad, dw)
        kn = jnp.pad(ks_b.reshape(nb_s, t_s, dw), ((0, 0), (0, page - t_s), (0, 0)))
        vn = jnp.pad(vs_b.reshape(nb_s, t_s, dw), ((0, 0), (0, page - t_s), (0, 0)))
        page_ids = (page_table.astype(jnp.int32) + l * n_pool).reshape(-1)
        o_ds = _diff_attn_decode(page_ids, lam, qbd, ck, cv, kn, vn, sub_w, row_slope, t_s, lam_init,
                                 _tile(n_pages, 8))
        o_ds = o_ds[:, :t_s].reshape(m_s, dw)
        sk.append(ks_f.reshape(nb_s, t_s, nh_d, DIFF_HEAD_DIM))
        sv.append(vs_f.reshape(nb_s, t_s, nh_d, DIFF_HEAD_DIM))
        s_state.append(_z_to_state(z_end_s))
        s_shift.append(pr_s3[:, -1])
        xs_res = _out_proj(xs, o_rs, o_ds, w_out_bf, g_ffn, router, tm_s)

        if not routed:
            wg = dense_w_gate[l // 2].astype(BF16)
            wu = dense_w_up[l // 2].astype(BF16)
            wd = dense_w_down[l // 2].astype(BF16)
            tf = _tile(wg.shape[1], 512)
            xp = _ffn_dense(xp_res[1], xp_res[0], wg, wu, wd, g_fin, last, _tile(m_p, 1024), tf)
            xs = _ffn_dense(xs_res[1], xs_res[0], wg, wu, wd, g_fin, last, tm_s, tf)
        else:
            wg = moe_w_gate[l // 2].astype(BF16)
            wu = moe_w_up[l // 2].astype(BF16)
            wd = moe_w_down[l // 2].astype(BF16)
            tf = _tile(wg.shape[2], 512)
            outs = []
            for (xn, hn, lg), m, tm_e, tm_c in ((xp_res, m_p, _tile(m_p, 512), tm_p),
                                               (xs_res, m_s, _tile(m_s, 128), tm_s)):
                plan = _route(lg, n_experts, tm_e)
                y2 = _moe_experts(plan, hn, wg, wu, wd, TOP_K * m, tm_e, tf)
                outs.append(_combine(xn, y2, g_fin, last, tm_c))
            xp, xs = outs

    y_prompt = xp.reshape(nb_p, t_p, d)
    y_sample = xs.reshape(nb_s, t_s, d)
    return (y_prompt, y_sample, jnp.stack(pk), jnp.stack(pv), jnp.stack(sk), jnp.stack(sv),
            jnp.stack(p_state), jnp.stack(p_shift), jnp.stack(s_state), jnp.stack(s_shift))
```

```python
import functools
import math

import jax
import jax.numpy as jnp
from jax import lax
from jax.experimental import pallas as pl
from jax.experimental.pallas import tpu as pltpu

F32 = jnp.float32
BF16 = jnp.bfloat16

RWKV_HEAD_DIM = 64
W_LORA = 64
A_LORA = 64
G_LORA = 128
DIFF_QK_DIM = 64
DIFF_HEAD_DIM = 128
TOP_K = 2
NORM_EPS = 1e-6
SUBLN_EPS = 1e-5
GN_EPS = 64e-5
LOG2E = math.log2(math.e)

LANES = 128
CHUNK = 64
PAIR = 2 * RWKV_HEAD_DIM
VMEM_LIMIT = 56 * 1024 * 1024


def _cp(sem, vmem=VMEM_LIMIT):
    return pltpu.CompilerParams(dimension_semantics=sem, vmem_limit_bytes=vmem)


def _dot(a, b):
    return jnp.dot(a, b, preferred_element_type=F32)


def _dot_nt(a, b):
    return lax.dot_general(a, b, (((1,), (1,)), ((), ())), preferred_element_type=F32)


def _dot_tn(a, b):
    return lax.dot_general(a, b, (((0,), (0,)), ((), ())), preferred_element_type=F32)


def _split2(x):
    hi = x.astype(BF16)
    lo = (x - hi.astype(F32)).astype(BF16)
    return hi, lo


def _split3(x):
    h1 = x.astype(BF16)
    r1 = x - h1.astype(F32)
    h2 = r1.astype(BF16)
    h3 = (r1 - h2.astype(F32)).astype(BF16)
    return h1, h2, h3


def _mm1(a, b, dot=_dot):
    return dot(a.astype(BF16), b.astype(BF16))


def _mm3(a, b, dot=_dot):
    ah, al = _split2(a)
    bh, bl = _split2(b)
    return dot(ah, bh) + dot(al, bh) + dot(ah, bl)


def _mm_exact_rhs(a, b_bf16, dot=_dot):
    ah, al = _split2(a)
    return dot(ah, b_bf16) + dot(al, b_bf16)


def _sigmoid(x):
    return 1.0 / (1.0 + jnp.exp(-x))


def _rms(x, g):
    ms = jnp.mean(x * x, axis=-1, keepdims=True)
    return x * lax.rsqrt(ms + NORM_EPS) * g


def _in_proj_kernel(x_ref, g_ref, w_ref, k_all_ref, v_all_ref, pr_ref, q1_ref, q2_ref, k_ref, v_ref,
                    kb_ref, vb_ref, *, rcols, dw):
    del k_all_ref, v_all_ref
    tm = x_ref.shape[0]
    nh = dw // DIFF_HEAD_DIM
    h = _rms(x_ref[...], g_ref[...]).astype(BF16)
    pr_ref[...] = _dot(h, w_ref[:, 0:rcols])
    q = _dot(h, w_ref[:, rcols:rcols + dw]) * (DIFF_QK_DIM ** -0.5 * LOG2E)
    lane = lax.broadcasted_iota(jnp.int32, (1, dw), 1) % DIFF_HEAD_DIM
    first = lane < DIFF_QK_DIM
    q1_ref[...] = jnp.where(first, q, 0.0).astype(BF16)
    q2_ref[...] = jnp.where(first, 0.0, q).astype(BF16)
    k = _dot(h, w_ref[:, rcols + dw:rcols + 2 * dw])
    v = _dot(h, w_ref[:, rcols + 2 * dw:rcols + 3 * dw])
    kb_ref[...] = k.astype(BF16)
    vb_ref[...] = v.astype(BF16)
    for hh in range(nh):
        cols = slice(hh * DIFF_HEAD_DIM, (hh + 1) * DIFF_HEAD_DIM)
        k_ref[0, pl.ds(hh, tm, stride=nh), :] = k[:, cols]
        v_ref[0, pl.ds(hh, tm, stride=nh), :] = v[:, cols]


def _in_proj(x2d, g, w_bf, rcols, dw, tm, layer, kv_all):
    m, d = x2d.shape
    n_in = w_bf.shape[1]
    nh = dw // DIFF_HEAD_DIM
    row = lambda i: (i, 0)
    const = lambda i: (0, 0)
    kv_shape = jax.ShapeDtypeStruct(kv_all[0].shape, F32)
    kv_spec = pl.BlockSpec((1, tm * nh, DIFF_HEAD_DIM), lambda i: (layer, i, 0))
    outs = (
        jax.ShapeDtypeStruct((m, rcols), F32),
        jax.ShapeDtypeStruct((m, dw), BF16),
        jax.ShapeDtypeStruct((m, dw), BF16),
        kv_shape, kv_shape,
        jax.ShapeDtypeStruct((m, dw), BF16),
        jax.ShapeDtypeStruct((m, dw), BF16),
    )
    return pl.pallas_call(
        functools.partial(_in_proj_kernel, rcols=rcols, dw=dw),
        grid=(m // tm,),
        in_specs=[pl.BlockSpec((tm, d), row), pl.BlockSpec((1, d), const), pl.BlockSpec((d, n_in), const),
                  pl.BlockSpec(memory_space=pl.ANY), pl.BlockSpec(memory_space=pl.ANY)],
        out_specs=[pl.BlockSpec((tm, rcols), row)] + [pl.BlockSpec((tm, dw), row)] * 2
        + [kv_spec, kv_spec] + [pl.BlockSpec((tm, dw), row)] * 2,
        out_shape=outs,
        input_output_aliases={3: 3, 4: 4},
        compiler_params=_cp(("parallel",)),
        name="in_proj",
    )(x2d, g, w_bf, *kv_all)


def _pair_chunks(items):
    c = items[0][0].shape[0]
    lane = lax.broadcasted_iota(jnp.int32, (1, PAIR), 1)
    first = lane < RWKV_HEAD_DIM

    def stack(x):
        return jnp.concatenate([jnp.where(first, x, 0.0), jnp.where(first, 0.0, x)], axis=0)

    ri = lax.broadcasted_iota(jnp.int32, (2 * c, 2 * c), 0)
    ci = lax.broadcasted_iota(jnp.int32, (2 * c, 2 * c), 1)
    eye = ri == ci
    strict = (ri % c) > (ci % c)
    incl = (ri % c) >= (ci % c)

    st = [tuple(stack(x) for x in it[:5]) for it in items]
    gs = [_mm1(jnp.concatenate([s[0], s[1]], axis=0), jnp.concatenate([s[2], s[3]], axis=0), _dot_nt)
          for s in st]
    a_ab = [jnp.where(strict, g[:2 * c, :2 * c], 0.0) for g in gs]
    a_ak = [jnp.where(strict, g[:2 * c, 2 * c:], 0.0) for g in gs]
    a_rb = [jnp.where(incl, g[2 * c:, :2 * c], 0.0) for g in gs]
    a_rk = [jnp.where(incl, g[2 * c:, 2 * c:], 0.0) for g in gs]
    akv = [_mm1(a, s[4]) for a, s in zip(a_ak, st)]
    tm = [jnp.where(eye, 1.0, 0.0) + a for a in a_ab]
    pw = [a.astype(BF16) for a in a_ab]
    span = 2
    while span < c:
        pw = [_dot(p, p).astype(BF16) for p in pw]
        tm = [t + _dot(t.astype(BF16), p) for t, p in zip(tm, pw)]
        span *= 2
    au = [_mm1(t, jnp.concatenate([s[0], x], axis=1)) for t, s, x in zip(tm, st, akv)]
    qy = [_mm1(a, u) for a, u in zip(a_rb, au)]
    rkv = [_mm1(a, s[4]) for a, s in zip(a_rk, st)]
    outs = []
    for it, s, u, w, x in zip(items, st, au, qy, rkv):
        pc = it[5]
        q_s = s[1] + w[:, :PAIR]
        y0_s = w[:, PAIR:] + x
        bh = s[2] * pc
        kh = s[3] * pc
        mt = jnp.where(eye, pc, 0.0) + _mm1(bh, u[:, :PAIR], _dot_tn)
        nt = _mm1(bh, u[:, PAIR:], _dot_tn) + _mm1(kh, s[4], _dot_tn)
        outs.append((q_s[:c] + q_s[c:], y0_s[:c] + y0_s[c:], mt, nt))
    return outs


def _rwkv_a_kernel(pr_ref, prev8_ref, prev0_ref, mu_ref, lwh_ref, lwl_ref, vec_ref, gup_ref, bd_ref,
                   q_ref, y0_ref, mt_ref, nt_ref, g_ref, bg_ref, *, tb, width, t_valid):
    j = pl.program_id(1)
    p_all = pr_ref[0]
    prev_first = jnp.where(j == 0, prev0_ref[0], prev8_ref[0][7:8, :])
    row = lax.broadcasted_iota(jnp.int32, (tb, 1), 0)
    p_prev = jnp.where(row == 0, prev_first, pltpu.roll(p_all, 1, axis=0))
    u = p_all + mu_ref[...] * (p_prev - p_all)
    r = u[:, 0:width]
    k = u[:, width:2 * width]
    v = u[:, 2 * width:3 * width]
    wa = u[:, 3 * width:3 * width + W_LORA + A_LORA]
    gd = u[:, 3 * width + W_LORA + A_LORA:]
    lane = lax.broadcasted_iota(jnp.int32, (1, W_LORA + A_LORA), 1)
    twa = jnp.where(lane < W_LORA, jnp.tanh(wa), wa)
    th, tl = _split2(twa)
    lora = _dot(th, lwh_ref[...]) + _dot(tl, lwh_ref[...]) + _dot(th, lwl_ref[...])
    w0, a0, k_k, k_a, r_k = (vec_ref[i:i + 1, :] for i in range(5))
    lw = (-math.exp(-0.5)) * _sigmoid(w0 + lora[:, :width])
    alpha = _sigmoid(a0 + lora[:, width:])
    g = _dot(_sigmoid(gd).astype(BF16), gup_ref[...])
    bd = bd_ref[...]
    kk = k * k_k
    kkn = kk / jnp.maximum(jnp.sqrt(_mm_exact_rhs(kk * kk, bd)), 1e-12)
    k2 = k * (1.0 + (alpha - 1.0) * k_a)
    bonus = _mm_exact_rhs(r * k2 * r_k, bd) * v
    if t_valid is not None:
        live = (j * tb + row) < t_valid
        lw = jnp.where(live, lw, 0.0)
        kkn = jnp.where(live, kkn, 0.0)
        k2 = jnp.where(live, k2, 0.0)
        v = jnp.where(live, v, 0.0)
    ri = lax.broadcasted_iota(jnp.int32, (tb, tb), 0)
    ci = lax.broadcasted_iota(jnp.int32, (tb, tb), 1)
    tri = jnp.where((ri // CHUNK == ci // CHUNK) & (ci <= ri), 1.0, 0.0).astype(BF16)
    l1, l2, l3 = _split3(lw)
    cum = _dot(tri, l1) + _dot(tri, l2) + _dot(tri, l3)
    e_in = jnp.exp(cum)
    e_ex = jnp.exp(cum - lw)
    e_neg = jnp.exp(-cum)
    rt = r * e_in
    at = -kkn * e_ex
    bt = kkn * alpha * e_neg
    kt = k2 * e_neg
    g_ref[0] = g
    bg_ref[0] = bonus * g
    items, where = [], []
    for c in range(tb // CHUNK):
        rows = slice(c * CHUNK, (c + 1) * CHUNK)
        pc = e_in[(c + 1) * CHUNK - 1:(c + 1) * CHUNK, :]
        for p in range(width // PAIR):
            cols = slice(p * PAIR, (p + 1) * PAIR)
            items.append((at[rows, cols], rt[rows, cols], bt[rows, cols], kt[rows, cols], v[rows, cols],
                          pc[:, cols]))
            where.append((c, p, rows, cols))
    for (c, p, rows, cols), (q, y0, mt, nt) in zip(where, _pair_chunks(items)):
        q_ref[0, rows, cols] = q
        y0_ref[0, rows, cols] = y0
        mt_ref[0, c, p] = mt
        nt_ref[0, c, p] = nt


def _rwkv_b_kernel(q_ref, y0_ref, mt_ref, nt_ref, g_ref, bg_ref, z0_ref, ln_ref, bd_ref,
                   o_ref, zout_ref, z_scr, y_scr, *, tb, width):
    j = pl.program_id(1)

    @pl.when(j == 0)
    def _():
        z_scr[...] = z0_ref[0]

    for c in range(tb // CHUNK):
        rows = slice(c * CHUNK, (c + 1) * CHUNK)
        for p in range(width // PAIR):
            cols = slice(p * PAIR, (p + 1) * PAIR)
            z = z_scr[p]
            y_scr[rows, cols] = _mm3(q_ref[0, rows, cols], z) + y0_ref[0, rows, cols]
            z_scr[p] = _mm3(mt_ref[0, c, p], z) + nt_ref[0, c, p]
    y = y_scr[...]
    bd = bd_ref[...]
    inv = 1.0 / RWKV_HEAD_DIM
    mean = _mm_exact_rhs(y, bd) * inv
    d = y - mean
    var = _mm_exact_rhs(d * d, bd) * inv
    yn = d * lax.rsqrt(var + GN_EPS) * ln_ref[0:1, :] + ln_ref[1:2, :]
    o_ref[0] = (yn * g_ref[0] + bg_ref[0]).astype(BF16)

    @pl.when(j == pl.num_programs(1) - 1)
    def _():
        zout_ref[0] = z_scr[...]


def _rwkv_mix(pr, prev_row, z0, rw, t_valid, tb_a, tb):
    ns, t, rcols = pr.shape
    width = rw["bd"].shape[0]
    n_pair = width // PAIR
    c3 = lambda n, j: (n, j, 0)
    const2 = lambda n, j: (0, 0)
    tok_shape = jax.ShapeDtypeStruct((ns, t, width), F32)
    mat_shape = jax.ShapeDtypeStruct((ns, t // CHUNK, n_pair, PAIR, PAIR), F32)
    tok = pl.BlockSpec((1, tb_a, width), c3)
    mat = pl.BlockSpec((1, tb_a // CHUNK, n_pair, PAIR, PAIR), lambda n, j: (n, j, 0, 0, 0))
    q, y0, mt, nt, g, bg = pl.pallas_call(
        functools.partial(_rwkv_a_kernel, tb=tb_a, width=width, t_valid=t_valid),
        grid=(ns, t // tb_a),
        in_specs=[
            pl.BlockSpec((1, tb_a, rcols), c3),
            pl.BlockSpec((1, 8, rcols), lambda n, j: (n, jnp.maximum(j * (tb_a // 8) - 1, 0), 0)),
            pl.BlockSpec((1, 1, rcols), lambda n, j: (n, 0, 0)),
            pl.BlockSpec((1, rcols), const2),
            pl.BlockSpec(rw["lora_hi"].shape, const2),
            pl.BlockSpec(rw["lora_lo"].shape, const2),
            pl.BlockSpec(rw["vec"].shape, const2),
            pl.BlockSpec(rw["g_up"].shape, const2),
            pl.BlockSpec((width, width), const2),
        ],
        out_specs=[tok, tok, mat, mat, tok, tok],
        out_shape=(tok_shape, tok_shape, mat_shape, mat_shape, tok_shape, tok_shape),
        compiler_params=_cp(("parallel", "parallel")),
        name="rwkv_chunk_summaries",
    )(pr, pr, prev_row, rw["mu"], rw["lora_hi"], rw["lora_lo"], rw["vec"], rw["g_up"], rw["bd"])
    zspec = pl.BlockSpec((1, n_pair, PAIR, PAIR), lambda n, j: (n, 0, 0, 0))
    tok = pl.BlockSpec((1, tb, width), c3)
    mat = pl.BlockSpec((1, tb // CHUNK, n_pair, PAIR, PAIR), lambda n, j: (n, j, 0, 0, 0))
    o, z_end = pl.pallas_call(
        functools.partial(_rwkv_b_kernel, tb=tb, width=width),
        grid=(ns, t // tb),
        in_specs=[tok, tok, mat, mat, tok, tok, zspec,
                  pl.BlockSpec((2, width), const2), pl.BlockSpec((width, width), const2)],
        out_specs=[tok, zspec],
        out_shape=(jax.ShapeDtypeStruct((ns, t, width), BF16),
                   jax.ShapeDtypeStruct((ns, n_pair, PAIR, PAIR), F32)),
        scratch_shapes=[pltpu.VMEM((n_pair, PAIR, PAIR), F32), pltpu.VMEM((tb, width), F32)],
        compiler_params=_cp(("parallel", "arbitrary")),
        name="rwkv_state_pass",
    )(q, y0, mt, nt, g, bg, z0, rw["ln"], rw["bd"])
    return o, z_end


def _state_to_z(s):
    ns, nh, dv, dk = s.shape
    st = jnp.swapaxes(s, -1, -2).reshape(ns, nh // 2, 2, dk, dv)
    z = jnp.zeros((ns, nh // 2, 2, dk, 2, dv), s.dtype)
    z = z.at[:, :, 0, :, 0, :].set(st[:, :, 0]).at[:, :, 1, :, 1, :].set(st[:, :, 1])
    return z.reshape(ns, nh // 2, 2 * dk, 2 * dv)


def _z_to_state(z):
    ns, npair = z.shape[:2]
    d = RWKV_HEAD_DIM
    z6 = z.reshape(ns, npair, 2, d, 2, d)
    st = jnp.stack([z6[:, :, 0, :, 0, :], z6[:, :, 1, :, 1, :]], axis=2)
    return jnp.swapaxes(st, -1, -2).reshape(ns, npair * 2, d, d)


def _diff_flash_kernel(lam_ref, q1_ref, q2_ref, k_ref, v_ref, w_ref, o_ref, *, tq, lam_init):
    h = pl.program_id(1)
    qi = pl.program_id(2)
    slope = lam_ref[1 + h]
    rel = (lax.broadcasted_iota(jnp.int32, (tq, tq), 0) - lax.broadcasted_iota(jnp.int32, (tq, tq), 1))
    bias = -slope * rel.astype(F32)
    q1 = q1_ref[...]
    q2 = q2_ref[...]

    def block(kj, carry, bias_blk):
        start = pl.multiple_of(kj * tq, tq)
        k = k_ref[pl.ds(start, tq), :]
        v = v_ref[pl.ds(start, tq), :]
        off = -slope * jnp.full((1, 1), (qi - kj) * tq, jnp.int32).astype(F32)
        out = []
        for q, (m, l, a) in zip((q1, q2), carry):
            s = _dot_nt(q, k) + bias_blk
            m_new = jnp.maximum(m, jnp.max(s, axis=-1, keepdims=True) + off)
            p = jnp.exp2(s - (m_new - off))
            scale = jnp.exp2(m - m_new)
            l = scale * l + jnp.sum(p, axis=-1, keepdims=True)
            a = scale * a + _dot(p.astype(BF16), v)
            out.append((m_new, l, a))
        return tuple(out)

    init = tuple((jnp.full((tq, 1), -jnp.inf, F32), jnp.zeros((tq, 1), F32),
                  jnp.zeros((tq, DIFF_HEAD_DIM), F32)) for _ in range(2))
    carry = lax.fori_loop(0, qi, lambda kj, c: block(kj, c, bias), init)
    (_, l1, a1), (_, l2, a2) = block(qi, carry, jnp.where(rel >= 0, bias, -jnp.inf))
    o = a1 / l1 - lam_ref[0] * (a2 / l2)
    o = o * lax.rsqrt(jnp.mean(o * o, axis=-1, keepdims=True) + SUBLN_EPS)
    o_ref[...] = (o * w_ref[...] * (1.0 - lam_init)).astype(BF16)


def _diff_attn_prompt(lam, q1, q2, kb, vb, subln_w, ns, t, lam_init, tq):
    m, dw = q1.shape
    nh = dw // DIFF_HEAD_DIM
    nq = t // tq
    qmap = lambda n, h, i: (n * nq + i, h)
    kmap = lambda n, h, i: (n, h)
    blk = (tq, DIFF_HEAD_DIM)
    return pl.pallas_call(
        functools.partial(_diff_flash_kernel, tq=tq, lam_init=lam_init),
        grid=(ns, nh, nq),
        in_specs=[pl.BlockSpec(memory_space=pltpu.SMEM),
                  pl.BlockSpec(blk, qmap), pl.BlockSpec(blk, qmap),
                  pl.BlockSpec((t, DIFF_HEAD_DIM), kmap), pl.BlockSpec((t, DIFF_HEAD_DIM), kmap),
                  pl.BlockSpec((1, DIFF_HEAD_DIM), lambda n, h, i: (0, 0))],
        out_specs=pl.BlockSpec(blk, qmap),
        out_shape=jax.ShapeDtypeStruct((m, dw), BF16),
        compiler_params=_cp(("parallel", "parallel", "parallel")),
        name="diff_attn_prompt",
    )(lam, q1, q2, kb, vb, subln_w)


def _diff_decode_kernel(pt_ref, lam_ref, q_ref, *refs, n_grp, page, n_past, t_dec, t_pad, nh, lam_init):
    k_refs = refs[:n_grp]
    v_refs = refs[n_grp:2 * n_grp]
    kn_ref, vn_ref, w_ref, sl_ref, o_ref, m_scr, l_scr, acc = refs[2 * n_grp:]
    g = pl.program_id(1)
    hr = 2 * t_pad
    rows = nh * hr

    @pl.when(g == 0)
    def _():
        m_scr[...] = jnp.full(m_scr.shape, -jnp.inf, F32)
        l_scr[...] = jnp.zeros(l_scr.shape, F32)
        acc[...] = jnp.zeros(acc.shape, F32)

    q = q_ref[0]
    trow = lax.broadcasted_iota(jnp.int32, (rows, 1), 0) % t_pad
    slope = sl_ref[...]

    def head(ref, hh):
        return ref[0, pl.ds(hh, page, stride=nh), :].astype(BF16)

    def update(s, vals):
        m_old = m_scr[...]
        m_new = jnp.maximum(m_old, jnp.max(s, axis=-1, keepdims=True))
        p = jnp.exp2(s - m_new)
        scale = jnp.exp2(m_old - m_new)
        l_scr[...] = scale * l_scr[...] + jnp.sum(p, axis=-1, keepdims=True)
        pv = []
        for hh in range(nh):
            part = None
            for i, vmat in enumerate(vals[hh]):
                d = _dot(p[hh * hr:(hh + 1) * hr, i * page:(i + 1) * page].astype(BF16), vmat)
                part = d if part is None else part + d
            pv.append(part)
        acc[...] = scale * acc[...] + jnp.concatenate(pv, axis=0)
        m_scr[...] = m_new

    s = jnp.concatenate(
        [jnp.concatenate([_dot_nt(q[hh * hr:(hh + 1) * hr], head(k_refs[i], hh)) for hh in range(nh)], axis=0)
         for i in range(n_grp)], axis=1)
    kpos = g * (n_grp * page) + lax.broadcasted_iota(jnp.int32, (1, n_grp * page), 1)
    s = s - slope * ((n_past + trow) - kpos).astype(F32)
    update(s, [[head(v_refs[i], hh) for i in range(n_grp)] for hh in range(nh)])

    @pl.when(g == pl.num_programs(1) - 1)
    def _():
        c = lax.broadcasted_iota(jnp.int32, (1, page), 1)
        ok = (c <= trow) & (c < t_dec)
        sn = jnp.concatenate([_dot_nt(q[hh * hr:(hh + 1) * hr], kn_ref[0, hh]) for hh in range(nh)], axis=0)
        sn = jnp.where(ok, sn - slope * (trow - c).astype(F32), -jnp.inf)
        update(sn, [[vn_ref[0, hh]] for hh in range(nh)])
        o = acc[...] / l_scr[...]
        w = w_ref[...]
        for hh in range(nh):
            d = o[hh * hr:hh * hr + t_pad] - lam_ref[0] * o[hh * hr + t_pad:(hh + 1) * hr]
            d = d * lax.rsqrt(jnp.mean(d * d, axis=-1, keepdims=True) + SUBLN_EPS)
            o_ref[0, :, hh * DIFF_HEAD_DIM:(hh + 1) * DIFF_HEAD_DIM] = (d * w * (1.0 - lam_init)).astype(BF16)


def _diff_attn_decode(page_ids, lam, qh, cache_k, cache_v, kn, vn, subln_w, row_slope, t_dec, lam_init,
                      n_grp):
    nb, rows, hd = qh.shape
    nh = kn.shape[1]
    page = kn.shape[2]
    n_pages = page_ids.shape[0] // nb
    t_pad = rows // (2 * nh)
    ng = n_pages // n_grp

    def page_spec(i):
        return pl.BlockSpec((1, page * nh, hd), lambda b, g, pt: (pt[b * n_pages + g * n_grp + i], 0, 0))

    per_b = lambda b, g, pt: (b, 0, 0)
    per_b4 = lambda b, g, pt: (b, 0, 0, 0)
    grid_spec = pltpu.PrefetchScalarGridSpec(
        num_scalar_prefetch=1,
        grid=(nb, ng),
        in_specs=[pl.BlockSpec(memory_space=pltpu.SMEM), pl.BlockSpec((1, rows, hd), per_b)]
        + [page_spec(i) for i in range(n_grp)] * 2
        + [pl.BlockSpec((1, nh, page, hd), per_b4), pl.BlockSpec((1, nh, page, hd), per_b4),
           pl.BlockSpec((1, hd), lambda b, g, pt: (0, 0)),
           pl.BlockSpec((rows, 1), lambda b, g, pt: (0, 0))],
        out_specs=pl.BlockSpec((1, t_pad, nh * hd), per_b),
        scratch_shapes=[pltpu.VMEM((rows, 1), F32), pltpu.VMEM((rows, 1), F32),
                        pltpu.VMEM((rows, hd), F32)],
    )
    return pl.pallas_call(
        functools.partial(_diff_decode_kernel, n_grp=n_grp, page=page, n_past=n_pages * page,
                          t_dec=t_dec, t_pad=t_pad, nh=nh, lam_init=lam_init),
        grid_spec=grid_spec,
        out_shape=jax.ShapeDtypeStruct((nb, t_pad, nh * hd), BF16),
        compiler_params=_cp(("parallel", "arbitrary")),
        name="diff_attn_decode",
    )(page_ids, lam, qh, *([cache_k] * n_grp), *([cache_v] * n_grp), kn, vn, subln_w, row_slope)


def _out_proj_kernel(x_ref, or_ref, od_ref, w_ref, g_ref, *rest, width, routed):
    if routed:
        rh_ref, rl_ref, xn_ref, hn_ref, lg_ref = rest
    else:
        xn_ref, hn_ref = rest
    xn = x_ref[...] + _dot(or_ref[...], w_ref[0:width, :]) + _dot(od_ref[...], w_ref[width:, :])
    xn_ref[...] = xn
    hn = _rms(xn, g_ref[...])
    hn_ref[...] = hn.astype(hn_ref.dtype)
    if routed:
        hh, hl = _split2(hn)
        lg_ref[...] = _dot(hh, rh_ref[...]) + _dot(hl, rh_ref[...]) + _dot(hh, rl_ref[...])


def _out_proj(x2d, o_r, o_d, w_bf, g, router, tm):
    m, d = x2d.shape
    width = o_r.shape[1]
    row = lambda i: (i, 0)
    const = lambda i: (0, 0)
    routed = router is not None
    in_specs = [pl.BlockSpec((tm, d), row), pl.BlockSpec((tm, width), row),
                pl.BlockSpec((tm, o_d.shape[1]), row), pl.BlockSpec(w_bf.shape, const),
                pl.BlockSpec((1, d), const)]
    args = [x2d, o_r, o_d, w_bf, g]
    out_specs = [pl.BlockSpec((tm, d), row), pl.BlockSpec((tm, d), row)]
    out_shape = [jax.ShapeDtypeStruct((m, d), F32),
                 jax.ShapeDtypeStruct((m, d), F32 if routed else BF16)]
    if routed:
        in_specs += [pl.BlockSpec(router[0].shape, const)] * 2
        args += list(router)
        out_specs.append(pl.BlockSpec((tm, LANES), row))
        out_shape.append(jax.ShapeDtypeStruct((m, LANES), F32))
    return pl.pallas_call(
        functools.partial(_out_proj_kernel, width=width, routed=routed),
        grid=(m // tm,),
        in_specs=in_specs, out_specs=out_specs, out_shape=out_shape,
        compiler_params=_cp(("parallel",)),
        name="out_proj",
    )(*args)


def _swiglu_partial(h, wg, wu, wd):
    g = _dot(h, wg)
    u = _dot(h, wu)
    return _dot((g * _sigmoid(g) * u).astype(BF16), wd)


def _ffn_kernel(h_ref, x_ref, wg_ref, wu_ref, wd_ref, gf_ref, o_ref, acc, *, final_norm):
    f = pl.program_id(1)
    part = _swiglu_partial(h_ref[...], wg_ref[...], wu_ref[...], wd_ref[...])

    @pl.when(f == 0)
    def _():
        acc[...] = part

    @pl.when(f > 0)
    def _():
        acc[...] += part

    @pl.when(f == pl.num_programs(1) - 1)
    def _():
        y = x_ref[...] + acc[...]
        o_ref[...] = _rms(y, gf_ref[...]) if final_norm else y


def _ffn_dense(hn, x2d, wg, wu, wd, g_final, final_norm, tm, tf):
    m, d = x2d.shape
    dff = wg.shape[1]
    row = lambda i, f: (i, 0)
    return pl.pallas_call(
        functools.partial(_ffn_kernel, final_norm=final_norm),
        grid=(m // tm, dff // tf),
        in_specs=[pl.BlockSpec((tm, d), row), pl.BlockSpec((tm, d), row),
                  pl.BlockSpec((d, tf), lambda i, f: (0, f)), pl.BlockSpec((d, tf), lambda i, f: (0, f)),
                  pl.BlockSpec((tf, d), lambda i, f: (f, 0)), pl.BlockSpec((1, d), lambda i, f: (0, 0))],
        out_specs=pl.BlockSpec((tm, d), row),
        out_shape=jax.ShapeDtypeStruct((m, d), F32),
        scratch_shapes=[pltpu.VMEM((tm, d), F32)],
        compiler_params=_cp(("parallel", "arbitrary")),
        name="ffn_dense",
    )(hn, x2d, wg, wu, wd, g_final)


def _moe_kernel(te_ref, nv_ref, nt_ref, tok_ref, dst_ref, h_hbm, gate_ref, wg_ref, wu_ref, wd_ref,
                o_hbm, xg, xb, acc, sem_in, sem_out, *, tm):
    i = pl.program_id(0)
    f = pl.program_id(1)
    base = i * tm

    @pl.when(i < nt_ref[0])
    def _():
        @pl.when(f == 0)
        def _():
            def issue(r, carry):
                pltpu.make_async_copy(h_hbm.at[pl.ds(tok_ref[base + r], 1)], xg.at[pl.ds(r, 1)],
                                      sem_in).start()
                return carry
            lax.fori_loop(0, tm, issue, 0, unroll=8)
            pltpu.make_async_copy(h_hbm.at[pl.ds(0, tm)], xg, sem_in).wait()
            xb[...] = xg[...].astype(BF16)

        part = _swiglu_partial(xb[...], wg_ref[0], wu_ref[0], wd_ref[0])

        @pl.when(f == 0)
        def _():
            acc[...] = part

        @pl.when(f > 0)
        def _():
            acc[...] += part

        @pl.when(f == pl.num_programs(1) - 1)
        def _():
            n_valid = nv_ref[i]
            xg[...] = acc[...] * gate_ref[...]

            def issue(r, carry):
                pltpu.make_async_copy(xg.at[pl.ds(r, 1)], o_hbm.at[pl.ds(dst_ref[base + r], 1)],
                                      sem_out).start()
                return carry
            lax.fori_loop(0, n_valid, issue, 0)
            size = tm
            while size >= 1:
                @pl.when((n_valid & size) != 0)
                def _(size=size):
                    pltpu.make_async_copy(xg.at[pl.ds(0, size)], o_hbm.at[pl.ds(0, size)], sem_out).wait()
                size //= 2


def _moe_experts(plan, hn, wg, wu, wd, n_rows_out, tm, tf):
    m, d = hn.shape
    dff = wg.shape[2]
    n_tiles = plan["tile_expert"].shape[0]
    nf = dff // tf

    def fblk(i, f, nt):
        return jnp.where(i < nt[0], f, nf - 1)

    wmap_in = lambda i, f, te, nv, nt, tok, dst: (te[i], 0, fblk(i, f, nt))
    wmap_out = lambda i, f, te, nv, nt, tok, dst: (te[i], fblk(i, f, nt), 0)
    grid_spec = pltpu.PrefetchScalarGridSpec(
        num_scalar_prefetch=5,
        grid=(n_tiles, dff // tf),
        in_specs=[pl.BlockSpec(memory_space=pl.ANY),
                  pl.BlockSpec((tm, 1), lambda i, f, te, nv, nt, tok, dst: (i, 0)),
                  pl.BlockSpec((1, d, tf), wmap_in), pl.BlockSpec((1, d, tf), wmap_in),
                  pl.BlockSpec((1, tf, d), wmap_out)],
        out_specs=pl.BlockSpec(memory_space=pl.ANY),
        scratch_shapes=[pltpu.VMEM((tm, d), F32), pltpu.VMEM((tm, d), BF16), pltpu.VMEM((tm, d), F32),
                        pltpu.SemaphoreType.DMA, pltpu.SemaphoreType.DMA],
    )
    return pl.pallas_call(
        functools.partial(_moe_kernel, tm=tm),
        grid_spec=grid_spec,
        out_shape=jax.ShapeDtypeStruct((n_rows_out, d), F32),
        compiler_params=_cp(("arbitrary", "arbitrary")),
        name="moe_experts",
    )(plan["tile_expert"], plan["tile_valid"], plan["n_tiles"], plan["row_token"], plan["row_dst"],
      hn, plan["row_gate"], wg, wu, wd)


def _route(logits, n_experts, tm):
    m = logits.shape[0]
    top_v, top_i = lax.top_k(logits[:, :n_experts], TOP_K)
    gates = jax.nn.softmax(top_v, axis=-1)
    e_flat = top_i.reshape(-1).astype(jnp.int32)
    n_assign = m * TOP_K
    order = jnp.argsort(e_flat, stable=True).astype(jnp.int32)
    counts = jnp.sum((e_flat[:, None] == jnp.arange(n_experts, dtype=jnp.int32)[None, :])
                     .astype(jnp.int32), axis=0)
    group_start = jnp.cumsum(counts) - counts
    tiles_per = (counts + tm - 1) // tm
    tile_end = jnp.cumsum(tiles_per)
    tile_start = tile_end - tiles_per
    n_tiles_max = n_assign // tm + n_experts
    n_rows = n_tiles_max * tm
    tiles = jnp.arange(n_tiles_max, dtype=jnp.int32)
    used = tiles < tile_end[-1]
    tile_expert = jnp.sum((jnp.minimum(tiles, tile_end[-1] - 1)[:, None] >= tile_end[None, :])
                          .astype(jnp.int32), axis=1)
    filled = jnp.clip(counts[tile_expert] - (tiles - tile_start[tile_expert]) * tm, 0, tm)
    tile_valid = jnp.where(used, filled, 0).astype(jnp.int32)
    rows = jnp.arange(n_rows, dtype=jnp.int32)
    r_tile = rows // tm
    r_exp = tile_expert[r_tile]
    r_idx = rows - tile_start[r_exp] * tm
    live = (rows - r_tile * tm) < tile_valid[r_tile]
    a = order[jnp.clip(group_start[r_exp] + r_idx, 0, n_assign - 1)]
    token = a // TOP_K
    row_token = jnp.where(live, token, 0)
    row_dst = jnp.where(live, (a % TOP_K) * m + token, 0)
    row_gate = jnp.where(live, gates.reshape(-1)[a], 0.0)
    return {
        "tile_expert": tile_expert.astype(jnp.int32),
        "tile_valid": tile_valid,
        "n_tiles": tile_end[-1:].astype(jnp.int32),
        "row_token": row_token,
        "row_dst": row_dst,
        "row_gate": row_gate.reshape(n_rows, 1),
    }


def _combine_kernel(x_ref, y0_ref, y1_ref, g_ref, o_ref, *, final_norm):
    y = x_ref[...] + y0_ref[...] + y1_ref[...]
    o_ref[...] = _rms(y, g_ref[...]) if final_norm else y


def _combine(x2d, y2, g_final, final_norm, tm):
    m, d = x2d.shape
    nblk = m // tm
    return pl.pallas_call(
        functools.partial(_combine_kernel, final_norm=final_norm),
        grid=(nblk,),
        in_specs=[pl.BlockSpec((tm, d), lambda i: (i, 0)), pl.BlockSpec((tm, d), lambda i: (i, 0)),
                  pl.BlockSpec((tm, d), lambda i: (i + nblk, 0)), pl.BlockSpec((1, d), lambda i: (0, 0))],
        out_specs=pl.BlockSpec((tm, d), lambda i: (i, 0)),
        out_shape=jax.ShapeDtypeStruct((m, d), F32),
        compiler_params=_cp(("parallel",)),
        name="moe_combine",
    )(x2d, y2, y2, g_final)


def _tile(m, pref):
    t = min(m, pref)
    while m % t:
        t //= 2
    return t


def _block_diag_ones(width, blk):
    i = jnp.arange(width) // blk
    return (i[:, None] == i[None, :]).astype(BF16)


def _rwkv_params(l, shift_mu, w0, w_up, a0, a_up, g_up, k_k, k_a, r_k, lnx_w, lnx_b):
    width = w0.shape[1]
    lora = jnp.zeros((W_LORA + A_LORA, 2 * width), F32)
    lora = lora.at[:W_LORA, :width].set(w_up[l]).at[W_LORA:, width:].set(a_up[l])
    lora_hi = lora.astype(BF16)
    lora_lo = (lora - lora_hi.astype(F32)).astype(BF16)
    vec = jnp.zeros((8, width), F32)
    vec = vec.at[0].set(w0[l]).at[1].set(a0[l]).at[2].set(k_k[l]).at[3].set(k_a[l])
    vec = vec.at[4].set(r_k[l].reshape(-1))
    return {
        "mu": shift_mu[l][None, :], "lora_hi": lora_hi, "lora_lo": lora_lo, "vec": vec,
        "g_up": g_up[l].astype(BF16), "bd": _block_diag_ones(width, RWKV_HEAD_DIM),
        "ln": jnp.stack([lnx_w[l], lnx_b[l]]),
    }


def kernel(x_prompt, x_sample, cache_k, cache_v, state_wkv, state_shift, page_table, norm_mix, w_in, shift_mu, w0, w_up, a0, a_up, g_up, k_k, k_a, r_k, lnx_w, lnx_b, lam_q1, lam_k1, lam_q2, lam_k2, subln_w, w_out, norm_ffn, dense_w_gate, dense_w_up, dense_w_down, router_w, moe_w_gate, moe_w_up, moe_w_down, norm_final):
    depth = w_in.shape[0]
    nb_p, t_p, d = x_prompt.shape
    nb_s, t_s, _ = x_sample.shape
    width = w0.shape[1]
    rcols = shift_mu.shape[1]
    dw = (w_in.shape[2] - rcols) // 3
    nh_d = dw // DIFF_HEAD_DIM
    nh_r = width // RWKV_HEAD_DIM
    n_pool, page = cache_k.shape[1], cache_k.shape[2]
    n_pages = page_table.shape[1]
    n_experts = router_w.shape[2]
    m_p = nb_p * t_p
    m_s = nb_s * t_s

    xp = x_prompt.reshape(m_p, d)
    xs = x_sample.reshape(m_s, d)
    ck = cache_k.reshape(depth * n_pool, page * nh_d, DIFF_HEAD_DIM)
    cv = cache_v.reshape(depth * n_pool, page * nh_d, DIFF_HEAD_DIM)
    t_pad = 8
    slopes = 2.0 ** (-8.0 * jnp.arange(1, nh_d + 1, dtype=F32) / nh_d) * LOG2E
    row_slope = jnp.repeat(slopes, 2 * t_pad)[:, None]

    tm_p = _tile(m_p, 512)
    tm_s = _tile(m_s, 128)
    kv_p = (jnp.zeros((depth, m_p * nh_d, DIFF_HEAD_DIM), F32),) * 2
    kv_s = (jnp.zeros((depth, m_s * nh_d, DIFF_HEAD_DIM), F32),) * 2
    p_state, p_shift, s_state, s_shift = [], [], [], []

    for l in range(depth):
        lam_init = 0.8 - 0.6 * math.exp(-0.3 * l)
        lam = (jnp.exp(jnp.sum(lam_q1[l] * lam_k1[l])) - jnp.exp(jnp.sum(lam_q2[l] * lam_k2[l]))
               + lam_init).reshape(1).astype(F32)
        lam = jnp.concatenate([lam, slopes])
        rw = _rwkv_params(l, shift_mu, w0, w_up, a0, a_up, g_up, k_k, k_a, r_k, lnx_w, lnx_b)
        w_in_bf = w_in[l].astype(BF16)
        w_out_bf = w_out[l].astype(BF16)
        g_mix = norm_mix[l][None, :]
        g_ffn = norm_ffn[l][None, :]
        g_fin = norm_final[None, :]
        sub_w = subln_w[l][None, :]
        last = l == depth - 1
        routed = l % 2 == 1
        router = None
        if routed:
            rt = jnp.zeros((d, LANES), F32).at[:, :n_experts].set(router_w[l // 2])
            rt_hi = rt.astype(BF16)
            router = (rt_hi, (rt - rt_hi.astype(F32)).astype(BF16))

        pr, q1, q2, k_f, v_f, k_b, v_b = _in_proj(xp, g_mix, w_in_bf, rcols, dw, tm_p, l, kv_p)
        kv_p = (k_f, v_f)
        pr3 = pr.reshape(nb_p, t_p, rcols)
        z0 = jnp.zeros((nb_p, nh_r // 2, PAIR, PAIR), F32)
        o_r, z_end = _rwkv_mix(pr3, jnp.zeros((nb_p, 1, rcols), F32), z0, rw, None,
                               _tile(t_p, 256), _tile(t_p, 256))
        o_d = _diff_attn_prompt(lam, q1, q2, k_b, v_b, sub_w, nb_p, t_p, lam_init, _tile(t_p, 512))
        p_state.append(_z_to_state(z_end))
        p_shift.append(pr3[:, -1])
        xp_res = _out_proj(xp, o_r.reshape(m_p, width), o_d, w_out_bf, g_ffn, router, tm_p)

        pr_s, q1s, q2s, ks_f, vs_f, ks_b, vs_b = _in_proj(xs, g_mix, w_in_bf, rcols, dw, tm_s, l, kv_s)
        kv_s = (ks_f, vs_f)
        pr_s3 = pr_s.reshape(nb_s, t_s, rcols)
        pr_pad = jnp.pad(pr_s3, ((0, 0), (0, CHUNK - t_s), (0, 0)))
        o_rs, z_end_s = _rwkv_mix(pr_pad, state_shift[l][:, None, :], _state_to_z(state_wkv[l]), rw,
                                  t_s, CHUNK, CHUNK)
        o_rs = o_rs[:, :t_s].reshape(m_s, width)
        qs = jnp.stack([q1s, q2s]).reshape(2, nb_s, t_s, nh_d, DIFF_HEAD_DIM)
        qs = jnp.pad(jnp.transpose(qs, (1, 3, 0, 2, 4)), ((0, 0),) * 3 + ((0, t_pad - t_s), (0, 0)))
        qh = qs.reshape(nb_s, nh_d * 2 * t_pad, DIFF_HEAD_DIM)

        def new_rows(z):
            z = jnp.swapaxes(z.reshape(nb_s, t_s, nh_d, DIFF_HEAD_DIM), 1, 2)
            return jnp.pad(z, ((0, 0), (0, 0), (0, page - t_s), (0, 0)))

        page_ids = (page_table.astype(jnp.int32) + l * n_pool).reshape(-1)
        o_ds = _diff_attn_decode(page_ids, lam, qh, ck, cv, new_rows(ks_b), new_rows(vs_b), sub_w,
                                 row_slope, t_s, lam_init, _tile(n_pages, 8))
        o_ds = o_ds[:, :t_s].reshape(m_s, dw)
        s_state.append(_z_to_state(z_end_s))
        s_shift.append(pr_s3[:, -1])
        xs_res = _out_proj(xs, o_rs, o_ds, w_out_bf, g_ffn, router, tm_s)

        if not routed:
            wg = dense_w_gate[l // 2].astype(BF16)
            wu = dense_w_up[l // 2].astype(BF16)
            wd = dense_w_down[l // 2].astype(BF16)
            tf = _tile(wg.shape[1], 512)
            xp = _ffn_dense(xp_res[1], xp_res[0], wg, wu, wd, g_fin, last, _tile(m_p, 1024), tf)
            xs = _ffn_dense(xs_res[1], xs_res[0], wg, wu, wd, g_fin, last, tm_s, tf)
        else:
            wg = moe_w_gate[l // 2].astype(BF16)
            wu = moe_w_up[l // 2].astype(BF16)
            wd = moe_w_down[l // 2].astype(BF16)
            tf = _tile(wg.shape[2], 512)
            outs = []
            for (xn, hn, lg), m, tm_e, tm_c in ((xp_res, m_p, _tile(m_p, 512), tm_p),
                                               (xs_res, m_s, _tile(m_s, 128), tm_s)):
                plan = _route(lg, n_experts, tm_e)
                y2 = _moe_experts(plan, hn, wg, wu, wd, TOP_K * m, tm_e, tf)
                outs.append(_combine(xn, y2, g_fin, last, tm_c))
            xp, xs = outs

    y_prompt = xp.reshape(nb_p, t_p, d)
    y_sample = xs.reshape(nb_s, t_s, d)
    kv5 = lambda z, nb, t: z.reshape(depth, nb, t, nh_d, DIFF_HEAD_DIM)
    return (y_prompt, y_sample, kv5(kv_p[0], nb_p, t_p), kv5(kv_p[1], nb_p, t_p),
            kv5(kv_s[0], nb_s, t_s), kv5(kv_s[1], nb_s, t_s),
            jnp.stack(p_state), jnp.stack(p_shift), jnp.stack(s_state), jnp.stack(s_shift))
```

```python
import functools
import math

import jax
import jax.numpy as jnp
from jax import lax
from jax.experimental import pallas as pl
from jax.experimental.pallas import tpu as pltpu

F32 = jnp.float32
BF16 = jnp.bfloat16

RWKV_HEAD_DIM = 64
W_LORA = 64
A_LORA = 64
G_LORA = 128
DIFF_QK_DIM = 64
DIFF_HEAD_DIM = 128
TOP_K = 2
NORM_EPS = 1e-6
SUBLN_EPS = 1e-5
GN_EPS = 64e-5
LOG2E = math.log2(math.e)

LANES = 128
CHUNK = 64
PAIR = 2 * RWKV_HEAD_DIM
VMEM_LIMIT = 56 * 1024 * 1024


def _cp(sem, vmem=VMEM_LIMIT):
    return pltpu.CompilerParams(dimension_semantics=sem, vmem_limit_bytes=vmem)


def _dot(a, b):
    return jnp.dot(a, b, preferred_element_type=F32)


def _dot_nt(a, b):
    return lax.dot_general(a, b, (((1,), (1,)), ((), ())), preferred_element_type=F32)


def _dot_tn(a, b):
    return lax.dot_general(a, b, (((0,), (0,)), ((), ())), preferred_element_type=F32)


def _split2(x):
    hi = x.astype(BF16)
    lo = (x - hi.astype(F32)).astype(BF16)
    return hi, lo


def _split3(x):
    h1 = x.astype(BF16)
    r1 = x - h1.astype(F32)
    h2 = r1.astype(BF16)
    h3 = (r1 - h2.astype(F32)).astype(BF16)
    return h1, h2, h3


def _mm1(a, b, dot=_dot):
    return dot(a.astype(BF16), b.astype(BF16))


def _mm3(a, b, dot=_dot):
    ah, al = _split2(a)
    bh, bl = _split2(b)
    return dot(ah, bh) + dot(al, bh) + dot(ah, bl)


def _mm_exact_rhs(a, b_bf16, dot=_dot):
    ah, al = _split2(a)
    return dot(ah, b_bf16) + dot(al, b_bf16)


def _sigmoid(x):
    return 1.0 / (1.0 + jnp.exp(-x))


def _rms(x, g):
    ms = jnp.mean(x * x, axis=-1, keepdims=True)
    return x * lax.rsqrt(ms + NORM_EPS) * g


def _in_proj_kernel(x_ref, g_ref, w_ref, *rest, rcols, dw, layer, first):
    if not first:
        rest = rest[2:]
    pr_ref, q1_ref, q2_ref, k_ref, v_ref, kb_ref, vb_ref = rest
    tm = x_ref.shape[0]
    nh = dw // DIFF_HEAD_DIM
    h = _rms(x_ref[...], g_ref[...]).astype(BF16)
    pr_ref[...] = _dot(h, w_ref[:, 0:rcols])
    q = _dot(h, w_ref[:, rcols:rcols + dw]) * (DIFF_QK_DIM ** -0.5 * LOG2E)
    lane = lax.broadcasted_iota(jnp.int32, (1, dw), 1) % DIFF_HEAD_DIM
    map1 = lane < DIFF_QK_DIM
    q1_ref[...] = jnp.where(map1, q, 0.0).astype(BF16)
    q2_ref[...] = jnp.where(map1, 0.0, q).astype(BF16)
    k = _dot(h, w_ref[:, rcols + dw:rcols + 2 * dw])
    v = _dot(h, w_ref[:, rcols + 2 * dw:rcols + 3 * dw])
    kb_ref[...] = k.astype(BF16)
    vb_ref[...] = v.astype(BF16)
    own = 0
    if first:
        own = layer
        for ll in range(k_ref.shape[0]):
            if ll != layer:
                k_ref[ll] = jnp.zeros(k_ref.shape[1:], F32)
                v_ref[ll] = jnp.zeros(v_ref.shape[1:], F32)
    for hh in range(nh):
        cols = slice(hh * DIFF_HEAD_DIM, (hh + 1) * DIFF_HEAD_DIM)
        k_ref[own, pl.ds(hh, tm, stride=nh), :] = k[:, cols]
        v_ref[own, pl.ds(hh, tm, stride=nh), :] = v[:, cols]


def _in_proj(x2d, g, w_bf, rcols, dw, tm, layer, depth, kv_all):
    m, d = x2d.shape
    n_in = w_bf.shape[1]
    nh = dw // DIFF_HEAD_DIM
    row = lambda i: (i, 0)
    const = lambda i: (0, 0)
    first = kv_all is None
    kv_shape = jax.ShapeDtypeStruct((depth, m * nh, DIFF_HEAD_DIM), F32)
    if first:
        kv_spec = pl.BlockSpec((depth, tm * nh, DIFF_HEAD_DIM), lambda i: (0, i, 0))
    else:
        kv_spec = pl.BlockSpec((1, tm * nh, DIFF_HEAD_DIM), lambda i: (layer, i, 0))
    outs = (
        jax.ShapeDtypeStruct((m, rcols), F32),
        jax.ShapeDtypeStruct((m, dw), BF16),
        jax.ShapeDtypeStruct((m, dw), BF16),
        kv_shape, kv_shape,
        jax.ShapeDtypeStruct((m, dw), BF16),
        jax.ShapeDtypeStruct((m, dw), BF16),
    )
    in_specs = [pl.BlockSpec((tm, d), row), pl.BlockSpec((1, d), const), pl.BlockSpec((d, n_in), const)]
    args = [x2d, g, w_bf]
    if not first:
        in_specs += [pl.BlockSpec(memory_space=pl.ANY)] * 2
        args += list(kv_all)
    return pl.pallas_call(
        functools.partial(_in_proj_kernel, rcols=rcols, dw=dw, layer=layer, first=first),
        grid=(m // tm,),
        in_specs=in_specs,
        out_specs=[pl.BlockSpec((tm, rcols), row)] + [pl.BlockSpec((tm, dw), row)] * 2
        + [kv_spec, kv_spec] + [pl.BlockSpec((tm, dw), row)] * 2,
        out_shape=outs,
        input_output_aliases={} if first else {3: 3, 4: 4},
        compiler_params=_cp(("parallel",)),
        name="in_proj",
    )(*args)


def _pair_chunks(items):
    c = items[0][0].shape[0]
    lane = lax.broadcasted_iota(jnp.int32, (1, PAIR), 1)
    first = lane < RWKV_HEAD_DIM

    def stack(x):
        return jnp.concatenate([jnp.where(first, x, 0.0), jnp.where(first, 0.0, x)], axis=0)

    ri = lax.broadcasted_iota(jnp.int32, (2 * c, 2 * c), 0)
    ci = lax.broadcasted_iota(jnp.int32, (2 * c, 2 * c), 1)
    eye = ri == ci
    strict = (ri % c) > (ci % c)
    incl = (ri % c) >= (ci % c)

    st = [tuple(stack(x) for x in it[:5]) for it in items]
    gs = [_mm1(jnp.concatenate([s[0], s[1]], axis=0), jnp.concatenate([s[2], s[3]], axis=0), _dot_nt)
          for s in st]
    a_ab = [jnp.where(strict, g[:2 * c, :2 * c], 0.0) for g in gs]
    a_ak = [jnp.where(strict, g[:2 * c, 2 * c:], 0.0) for g in gs]
    a_rb = [jnp.where(incl, g[2 * c:, :2 * c], 0.0) for g in gs]
    a_rk = [jnp.where(incl, g[2 * c:, 2 * c:], 0.0) for g in gs]
    akv = [_mm1(a, s[4]) for a, s in zip(a_ak, st)]
    tm = [jnp.where(eye, 1.0, 0.0) + a for a in a_ab]
    pw = [a.astype(BF16) for a in a_ab]
    span = 2
    while span < c:
        pw = [_dot(p, p).astype(BF16) for p in pw]
        tm = [t + _dot(t.astype(BF16), p) for t, p in zip(tm, pw)]
        span *= 2
    au = [_mm1(t, jnp.concatenate([s[0], x], axis=1)) for t, s, x in zip(tm, st, akv)]
    qy = [_mm1(a, u) for a, u in zip(a_rb, au)]
    rkv = [_mm1(a, s[4]) for a, s in zip(a_rk, st)]
    outs = []
    for it, s, u, w, x in zip(items, st, au, qy, rkv):
        pc = it[5]
        q_s = s[1] + w[:, :PAIR]
        y0_s = w[:, PAIR:] + x
        bh = s[2] * pc
        kh = s[3] * pc
        mt = jnp.where(eye, pc, 0.0) + _mm1(bh, u[:, :PAIR], _dot_tn)
        nt = _mm1(bh, u[:, PAIR:], _dot_tn) + _mm1(kh, s[4], _dot_tn)
        outs.append((q_s[:c] + q_s[c:], y0_s[:c] + y0_s[c:], mt, nt))
    return outs


def _rwkv_a_kernel(pr_ref, prev8_ref, prev0_ref, mu_ref, lwh_ref, lwl_ref, vec_ref, gup_ref, bd_ref,
                   q_ref, y0_ref, mt_ref, nt_ref, g_ref, bg_ref, *, tb, width, t_valid):
    j = pl.program_id(1)
    p_all = pr_ref[0]
    prev_first = jnp.where(j == 0, prev0_ref[0], prev8_ref[0][7:8, :])
    row = lax.broadcasted_iota(jnp.int32, (tb, 1), 0)
    p_prev = jnp.where(row == 0, prev_first, pltpu.roll(p_all, 1, axis=0))
    u = p_all + mu_ref[...] * (p_prev - p_all)
    r = u[:, 0:width]
    k = u[:, width:2 * width]
    v = u[:, 2 * width:3 * width]
    wa = u[:, 3 * width:3 * width + W_LORA + A_LORA]
    gd = u[:, 3 * width + W_LORA + A_LORA:]
    lane = lax.broadcasted_iota(jnp.int32, (1, W_LORA + A_LORA), 1)
    twa = jnp.where(lane < W_LORA, jnp.tanh(wa), wa)
    th, tl = _split2(twa)
    lora = _dot(th, lwh_ref[...]) + _dot(tl, lwh_ref[...]) + _dot(th, lwl_ref[...])
    w0, a0, k_k, k_a, r_k = (vec_ref[i:i + 1, :] for i in range(5))
    lw = (-math.exp(-0.5)) * _sigmoid(w0 + lora[:, :width])
    alpha = _sigmoid(a0 + lora[:, width:])
    g = _dot(_sigmoid(gd).astype(BF16), gup_ref[...])
    bd = bd_ref[...]
    kk = k * k_k
    kkn = kk / jnp.maximum(jnp.sqrt(_mm_exact_rhs(kk * kk, bd)), 1e-12)
    k2 = k * (1.0 + (alpha - 1.0) * k_a)
    bonus = _mm_exact_rhs(r * k2 * r_k, bd) * v
    if t_valid is not None:
        live = (j * tb + row) < t_valid
        lw = jnp.where(live, lw, 0.0)
        kkn = jnp.where(live, kkn, 0.0)
        k2 = jnp.where(live, k2, 0.0)
        v = jnp.where(live, v, 0.0)
    ri = lax.broadcasted_iota(jnp.int32, (tb, tb), 0)
    ci = lax.broadcasted_iota(jnp.int32, (tb, tb), 1)
    tri = jnp.where((ri // CHUNK == ci // CHUNK) & (ci <= ri), 1.0, 0.0).astype(BF16)
    l1, l2, l3 = _split3(lw)
    cum = _dot(tri, l1) + _dot(tri, l2) + _dot(tri, l3)
    e_in = jnp.exp(cum)
    e_ex = jnp.exp(cum - lw)
    e_neg = jnp.exp(-cum)
    rt = r * e_in
    at = -kkn * e_ex
    bt = kkn * alpha * e_neg
    kt = k2 * e_neg
    g_ref[0] = g
    bg_ref[0] = bonus * g
    items, where = [], []
    for c in range(tb // CHUNK):
        rows = slice(c * CHUNK, (c + 1) * CHUNK)
        pc = e_in[(c + 1) * CHUNK - 1:(c + 1) * CHUNK, :]
        for p in range(width // PAIR):
            cols = slice(p * PAIR, (p + 1) * PAIR)
            items.append((at[rows, cols], rt[rows, cols], bt[rows, cols], kt[rows, cols], v[rows, cols],
                          pc[:, cols]))
            where.append((c, p, rows, cols))
    for (c, p, rows, cols), (q, y0, mt, nt) in zip(where, _pair_chunks(items)):
        q_ref[0, rows, cols] = q
        y0_ref[0, rows, cols] = y0
        mt_ref[0, c, p] = mt
        nt_ref[0, c, p] = nt


def _rwkv_b_kernel(q_ref, y0_ref, mt_ref, nt_ref, g_ref, bg_ref, z0_ref, ln_ref, bd_ref,
                   o_ref, zout_ref, z_scr, y_scr, *, tb, width):
    j = pl.program_id(1)

    @pl.when(j == 0)
    def _():
        z_scr[...] = z0_ref[0]

    for c in range(tb // CHUNK):
        rows = slice(c * CHUNK, (c + 1) * CHUNK)
        for p in range(width // PAIR):
            cols = slice(p * PAIR, (p + 1) * PAIR)
            z = z_scr[p]
            y_scr[rows, cols] = _mm3(q_ref[0, rows, cols], z) + y0_ref[0, rows, cols]
            z_scr[p] = _mm3(mt_ref[0, c, p], z) + nt_ref[0, c, p]
    y = y_scr[...]
    bd = bd_ref[...]
    inv = 1.0 / RWKV_HEAD_DIM
    mean = _mm_exact_rhs(y, bd) * inv
    d = y - mean
    var = _mm_exact_rhs(d * d, bd) * inv
    yn = d * lax.rsqrt(var + GN_EPS) * ln_ref[0:1, :] + ln_ref[1:2, :]
    o_ref[0] = (yn * g_ref[0] + bg_ref[0]).astype(BF16)

    @pl.when(j == pl.num_programs(1) - 1)
    def _():
        zout_ref[0] = z_scr[...]


def _rwkv_mix(pr, prev_row, z0, rw, t_valid, tb_a, tb):
    ns, t, rcols = pr.shape
    width = rw["bd"].shape[0]
    n_pair = width // PAIR
    c3 = lambda n, j: (n, j, 0)
    const2 = lambda n, j: (0, 0)
    tok_shape = jax.ShapeDtypeStruct((ns, t, width), F32)
    mat_shape = jax.ShapeDtypeStruct((ns, t // CHUNK, n_pair, PAIR, PAIR), F32)
    tok = pl.BlockSpec((1, tb_a, width), c3)
    mat = pl.BlockSpec((1, tb_a // CHUNK, n_pair, PAIR, PAIR), lambda n, j: (n, j, 0, 0, 0))
    q, y0, mt, nt, g, bg = pl.pallas_call(
        functools.partial(_rwkv_a_kernel, tb=tb_a, width=width, t_valid=t_valid),
        grid=(ns, t // tb_a),
        in_specs=[
            pl.BlockSpec((1, tb_a, rcols), c3),
            pl.BlockSpec((1, 8, rcols), lambda n, j: (n, jnp.maximum(j * (tb_a // 8) - 1, 0), 0)),
            pl.BlockSpec((1, 1, rcols), lambda n, j: (n, 0, 0)),
            pl.BlockSpec((1, rcols), const2),
            pl.BlockSpec(rw["lora_hi"].shape, const2),
            pl.BlockSpec(rw["lora_lo"].shape, const2),
            pl.BlockSpec(rw["vec"].shape, const2),
            pl.BlockSpec(rw["g_up"].shape, const2),
            pl.BlockSpec((width, width), const2),
        ],
        out_specs=[tok, tok, mat, mat, tok, tok],
        out_shape=(tok_shape, tok_shape, mat_shape, mat_shape, tok_shape, tok_shape),
        compiler_params=_cp(("parallel", "parallel")),
        name="rwkv_chunk_summaries",
    )(pr, pr, prev_row, rw["mu"], rw["lora_hi"], rw["lora_lo"], rw["vec"], rw["g_up"], rw["bd"])
    zspec = pl.BlockSpec((1, n_pair, PAIR, PAIR), lambda n, j: (n, 0, 0, 0))
    tok = pl.BlockSpec((1, tb, width), c3)
    mat = pl.BlockSpec((1, tb // CHUNK, n_pair, PAIR, PAIR), lambda n, j: (n, j, 0, 0, 0))
    o, z_end = pl.pallas_call(
        functools.partial(_rwkv_b_kernel, tb=tb, width=width),
        grid=(ns, t // tb),
        in_specs=[tok, tok, mat, mat, tok, tok, zspec,
                  pl.BlockSpec((2, width), const2), pl.BlockSpec((width, width), const2)],
        out_specs=[tok, zspec],
        out_shape=(jax.ShapeDtypeStruct((ns, t, width), BF16),
                   jax.ShapeDtypeStruct((ns, n_pair, PAIR, PAIR), F32)),
        scratch_shapes=[pltpu.VMEM((n_pair, PAIR, PAIR), F32), pltpu.VMEM((tb, width), F32)],
        compiler_params=_cp(("parallel", "arbitrary")),
        name="rwkv_state_pass",
    )(q, y0, mt, nt, g, bg, z0, rw["ln"], rw["bd"])
    return o, z_end


def _state_to_z(s):
    ns, nh, dv, dk = s.shape
    st = jnp.swapaxes(s, -1, -2).reshape(ns, nh // 2, 2, dk, dv)
    z = jnp.zeros((ns, nh // 2, 2, dk, 2, dv), s.dtype)
    z = z.at[:, :, 0, :, 0, :].set(st[:, :, 0]).at[:, :, 1, :, 1, :].set(st[:, :, 1])
    return z.reshape(ns, nh // 2, 2 * dk, 2 * dv)


def _z_to_state(z):
    ns, npair = z.shape[:2]
    d = RWKV_HEAD_DIM
    z6 = z.reshape(ns, npair, 2, d, 2, d)
    st = jnp.stack([z6[:, :, 0, :, 0, :], z6[:, :, 1, :, 1, :]], axis=2)
    return jnp.swapaxes(st, -1, -2).reshape(ns, npair * 2, d, d)


def _diff_flash_kernel(lam_ref, q1_ref, q2_ref, k_ref, v_ref, w_ref, o_ref, *, tq, lam_init):
    h = pl.program_id(1)
    qi = pl.program_id(2)
    slope = lam_ref[1 + h]
    rel = (lax.broadcasted_iota(jnp.int32, (tq, tq), 0) - lax.broadcasted_iota(jnp.int32, (tq, tq), 1))
    bias = -slope * rel.astype(F32)
    q1 = q1_ref[...]
    q2 = q2_ref[...]

    def block(kj, carry, bias_blk):
        start = pl.multiple_of(kj * tq, tq)
        k = k_ref[pl.ds(start, tq), :]
        v = v_ref[pl.ds(start, tq), :]
        off = -slope * jnp.full((1, 1), (qi - kj) * tq, jnp.int32).astype(F32)
        out = []
        for q, (m, l, a) in zip((q1, q2), carry):
            s = _dot_nt(q, k) + bias_blk
            m_new = jnp.maximum(m, jnp.max(s, axis=-1, keepdims=True) + off)
            p = jnp.exp2(s - (m_new - off))
            scale = jnp.exp2(m - m_new)
            l = scale * l + jnp.sum(p, axis=-1, keepdims=True)
            a = scale * a + _dot(p.astype(BF16), v)
            out.append((m_new, l, a))
        return tuple(out)

    init = tuple((jnp.full((tq, 1), -jnp.inf, F32), jnp.zeros((tq, 1), F32),
                  jnp.zeros((tq, DIFF_HEAD_DIM), F32)) for _ in range(2))
    carry = lax.fori_loop(0, qi, lambda kj, c: block(kj, c, bias), init)
    (_, l1, a1), (_, l2, a2) = block(qi, carry, jnp.where(rel >= 0, bias, -jnp.inf))
    o = a1 / l1 - lam_ref[0] * (a2 / l2)
    o = o * lax.rsqrt(jnp.mean(o * o, axis=-1, keepdims=True) + SUBLN_EPS)
    o_ref[...] = (o * w_ref[...] * (1.0 - lam_init)).astype(BF16)


def _diff_attn_prompt(lam, q1, q2, kb, vb, subln_w, ns, t, lam_init, tq):
    m, dw = q1.shape
    nh = dw // DIFF_HEAD_DIM
    nq = t // tq
    qmap = lambda n, h, i: (n * nq + i, h)
    kmap = lambda n, h, i: (n, h)
    blk = (tq, DIFF_HEAD_DIM)
    return pl.pallas_call(
        functools.partial(_diff_flash_kernel, tq=tq, lam_init=lam_init),
        grid=(ns, nh, nq),
        in_specs=[pl.BlockSpec(memory_space=pltpu.SMEM),
                  pl.BlockSpec(blk, qmap), pl.BlockSpec(blk, qmap),
                  pl.BlockSpec((t, DIFF_HEAD_DIM), kmap), pl.BlockSpec((t, DIFF_HEAD_DIM), kmap),
                  pl.BlockSpec((1, DIFF_HEAD_DIM), lambda n, h, i: (0, 0))],
        out_specs=pl.BlockSpec(blk, qmap),
        out_shape=jax.ShapeDtypeStruct((m, dw), BF16),
        compiler_params=_cp(("parallel", "parallel", "parallel")),
        name="diff_attn_prompt",
    )(lam, q1, q2, kb, vb, subln_w)


def _diff_decode_kernel(pt_ref, lam_ref, q_ref, *refs, n_grp, page, n_past, t_dec, t_pad, nh, lam_init):
    k_refs = refs[:n_grp]
    v_refs = refs[n_grp:2 * n_grp]
    kn_ref, vn_ref, w_ref, sl_ref, o_ref, m_scr, l_scr, acc = refs[2 * n_grp:]
    g = pl.program_id(1)
    hr = 2 * t_pad
    rows = nh * hr

    @pl.when(g == 0)
    def _():
        m_scr[...] = jnp.full(m_scr.shape, -jnp.inf, F32)
        l_scr[...] = jnp.zeros(l_scr.shape, F32)
        acc[...] = jnp.zeros(acc.shape, F32)

    q = q_ref[0]
    trow = lax.broadcasted_iota(jnp.int32, (rows, 1), 0) % t_pad
    slope = sl_ref[...]

    def head(ref, hh):
        return ref[0, pl.ds(hh, page, stride=nh), :].astype(BF16)

    def update(s, vals):
        m_old = m_scr[...]
        m_new = jnp.maximum(m_old, jnp.max(s, axis=-1, keepdims=True))
        p = jnp.exp2(s - m_new)
        scale = jnp.exp2(m_old - m_new)
        l_scr[...] = scale * l_scr[...] + jnp.sum(p, axis=-1, keepdims=True)
        pv = []
        for hh in range(nh):
            part = None
            for i, vmat in enumerate(vals[hh]):
                d = _dot(p[hh * hr:(hh + 1) * hr, i * page:(i + 1) * page].astype(BF16), vmat)
                part = d if part is None else part + d
            pv.append(part)
        acc[...] = scale * acc[...] + jnp.concatenate(pv, axis=0)
        m_scr[...] = m_new

    s = jnp.concatenate(
        [jnp.concatenate([_dot_nt(q[hh * hr:(hh + 1) * hr], head(k_refs[i], hh)) for hh in range(nh)], axis=0)
         for i in range(n_grp)], axis=1)
    kpos = g * (n_grp * page) + lax.broadcasted_iota(jnp.int32, (1, n_grp * page), 1)
    s = s - slope * ((n_past + trow) - kpos).astype(F32)
    update(s, [[head(v_refs[i], hh) for i in range(n_grp)] for hh in range(nh)])

    @pl.when(g == pl.num_programs(1) - 1)
    def _():
        c = lax.broadcasted_iota(jnp.int32, (1, page), 1)
        ok = (c <= trow) & (c < t_dec)
        sn = jnp.concatenate([_dot_nt(q[hh * hr:(hh + 1) * hr], kn_ref[0, hh]) for hh in range(nh)], axis=0)
        sn = jnp.where(ok, sn - slope * (trow - c).astype(F32), -jnp.inf)
        update(sn, [[vn_ref[0, hh]] for hh in range(nh)])
        o = acc[...] / l_scr[...]
        w = w_ref[...]
        for hh in range(nh):
            d = o[hh * hr:hh * hr + t_pad] - lam_ref[0] * o[hh * hr + t_pad:(hh + 1) * hr]
            d = d * lax.rsqrt(jnp.mean(d * d, axis=-1, keepdims=True) + SUBLN_EPS)
            o_ref[0, :, hh * DIFF_HEAD_DIM:(hh + 1) * DIFF_HEAD_DIM] = (d * w * (1.0 - lam_init)).astype(BF16)


def _diff_attn_decode(page_ids, lam, qh, cache_k, cache_v, kn, vn, subln_w, row_slope, t_dec, lam_init,
                      n_grp):
    nb, rows, hd = qh.shape
    nh = kn.shape[1]
    page = kn.shape[2]
    n_pages = page_ids.shape[0] // nb
    t_pad = rows // (2 * nh)
    ng = n_pages // n_grp

    def page_spec(i):
        return pl.BlockSpec((1, page * nh, hd), lambda b, g, pt: (pt[b * n_pages + g * n_grp + i], 0, 0))

    per_b = lambda b, g, pt: (b, 0, 0)
    per_b4 = lambda b, g, pt: (b, 0, 0, 0)
    grid_spec = pltpu.PrefetchScalarGridSpec(
        num_scalar_prefetch=1,
        grid=(nb, ng),
        in_specs=[pl.BlockSpec(memory_space=pltpu.SMEM), pl.BlockSpec((1, rows, hd), per_b)]
        + [page_spec(i) for i in range(n_grp)] * 2
        + [pl.BlockSpec((1, nh, page, hd), per_b4), pl.BlockSpec((1, nh, page, hd), per_b4),
           pl.BlockSpec((1, hd), lambda b, g, pt: (0, 0)),
           pl.BlockSpec((rows, 1), lambda b, g, pt: (0, 0))],
        out_specs=pl.BlockSpec((1, t_pad, nh * hd), per_b),
        scratch_shapes=[pltpu.VMEM((rows, 1), F32), pltpu.VMEM((rows, 1), F32),
                        pltpu.VMEM((rows, hd), F32)],
    )
    return pl.pallas_call(
        functools.partial(_diff_decode_kernel, n_grp=n_grp, page=page, n_past=n_pages * page,
                          t_dec=t_dec, t_pad=t_pad, nh=nh, lam_init=lam_init),
        grid_spec=grid_spec,
        out_shape=jax.ShapeDtypeStruct((nb, t_pad, nh * hd), BF16),
        compiler_params=_cp(("parallel", "arbitrary")),
        name="diff_attn_decode",
    )(page_ids, lam, qh, *([cache_k] * n_grp), *([cache_v] * n_grp), kn, vn, subln_w, row_slope)


def _out_proj_kernel(x_ref, or_ref, od_ref, w_ref, g_ref, *rest, width, routed):
    if routed:
        rh_ref, rl_ref, xn_ref, hn_ref, lg_ref = rest
    else:
        xn_ref, hn_ref = rest
    xn = x_ref[...] + _dot(or_ref[...], w_ref[0:width, :]) + _dot(od_ref[...], w_ref[width:, :])
    xn_ref[...] = xn
    hn = _rms(xn, g_ref[...])
    hn_ref[...] = hn.astype(hn_ref.dtype)
    if routed:
        hh, hl = _split2(hn)
        lg_ref[...] = _dot(hh, rh_ref[...]) + _dot(hl, rh_ref[...]) + _dot(hh, rl_ref[...])


def _out_proj(x2d, o_r, o_d, w_bf, g, router, tm):
    m, d = x2d.shape
    width = o_r.shape[1]
    row = lambda i: (i, 0)
    const = lambda i: (0, 0)
    routed = router is not None
    in_specs = [pl.BlockSpec((tm, d), row), pl.BlockSpec((tm, width), row),
                pl.BlockSpec((tm, o_d.shape[1]), row), pl.BlockSpec(w_bf.shape, const),
                pl.BlockSpec((1, d), const)]
    args = [x2d, o_r, o_d, w_bf, g]
    out_specs = [pl.BlockSpec((tm, d), row), pl.BlockSpec((tm, d), row)]
    out_shape = [jax.ShapeDtypeStruct((m, d), F32),
                 jax.ShapeDtypeStruct((m, d), F32 if routed else BF16)]
    if routed:
        in_specs += [pl.BlockSpec(router[0].shape, const)] * 2
        args += list(router)
        out_specs.append(pl.BlockSpec((tm, LANES), row))
        out_shape.append(jax.ShapeDtypeStruct((m, LANES), F32))
    return pl.pallas_call(
        functools.partial(_out_proj_kernel, width=width, routed=routed),
        grid=(m // tm,),
        in_specs=in_specs, out_specs=out_specs, out_shape=out_shape,
        compiler_params=_cp(("parallel",)),
        name="out_proj",
    )(*args)


def _swiglu_partial(h, wg, wu, wd):
    g = _dot(h, wg)
    u = _dot(h, wu)
    return _dot((g * _sigmoid(g) * u).astype(BF16), wd)


def _ffn_kernel(h_ref, x_ref, wg_ref, wu_ref, wd_ref, gf_ref, o_ref, acc, *, final_norm):
    f = pl.program_id(1)
    part = _swiglu_partial(h_ref[...], wg_ref[...], wu_ref[...], wd_ref[...])

    @pl.when(f == 0)
    def _():
        acc[...] = part

    @pl.when(f > 0)
    def _():
        acc[...] += part

    @pl.when(f == pl.num_programs(1) - 1)
    def _():
        y = x_ref[...] + acc[...]
        o_ref[...] = _rms(y, gf_ref[...]) if final_norm else y


def _ffn_dense(hn, x2d, wg, wu, wd, g_final, final_norm, tm, tf):
    m, d = x2d.shape
    dff = wg.shape[1]
    row = lambda i, f: (i, 0)
    return pl.pallas_call(
        functools.partial(_ffn_kernel, final_norm=final_norm),
        grid=(m // tm, dff // tf),
        in_specs=[pl.BlockSpec((tm, d), row), pl.BlockSpec((tm, d), row),
                  pl.BlockSpec((d, tf), lambda i, f: (0, f)), pl.BlockSpec((d, tf), lambda i, f: (0, f)),
                  pl.BlockSpec((tf, d), lambda i, f: (f, 0)), pl.BlockSpec((1, d), lambda i, f: (0, 0))],
        out_specs=pl.BlockSpec((tm, d), row),
        out_shape=jax.ShapeDtypeStruct((m, d), F32),
        scratch_shapes=[pltpu.VMEM((tm, d), F32)],
        compiler_params=_cp(("parallel", "arbitrary")),
        name="ffn_dense",
    )(hn, x2d, wg, wu, wd, g_final)


def _moe_kernel(te_ref, nv_ref, nt_ref, src_ref, tok_ref, dst_ref, h_hbm, wg_ref, wu_ref, wd_ref,
                o_hbm, xg, xb, acc, yb, sem_in, sem_out, *, tm, nf):
    i = pl.program_id(0)
    f = pl.program_id(1)
    n_used = nt_ref[0]

    def gather(tile, slot):
        base = src_ref[tile]

        def issue(r, carry):
            pltpu.make_async_copy(h_hbm.at[pl.ds(tok_ref[base + r], 1)], xg.at[slot, pl.ds(r, 1)],
                                  sem_in.at[slot]).start()
            return carry
        lax.fori_loop(0, tm, issue, 0, unroll=8)

    def wait_scatter(n_rows):
        size = tm
        while size >= 1:
            @pl.when((n_rows & size) != 0)
            def _(size=size):
                pltpu.make_async_copy(yb.at[pl.ds(0, size)], o_hbm.at[pl.ds(0, size)], sem_out).wait()
            size //= 2

    @pl.when(i < n_used)
    def _():
        slot = i % 2

        @pl.when(f == 0)
        def _():
            @pl.when(i == 0)
            def _():
                gather(0, 0)
            pltpu.make_async_copy(h_hbm.at[pl.ds(0, tm)], xg.at[slot], sem_in.at[slot]).wait()
            xb[...] = xg[slot].astype(BF16)

        @pl.when((f == min(1, nf - 1)) & (i + 1 < n_used))
        def _():
            gather(i + 1, 1 - slot)

        part = _swiglu_partial(xb[...], wg_ref[0], wu_ref[0], wd_ref[0])

        @pl.when(f == 0)
        def _():
            acc[...] = part

        @pl.when(f > 0)
        def _():
            acc[...] += part

        @pl.when(f == nf - 1)
        def _():
            @pl.when(i > 0)
            def _():
                wait_scatter(nv_ref[jnp.maximum(i - 1, 0)])
            n_valid = nv_ref[i]
            base = src_ref[i]
            yb[...] = acc[...]

            def issue(r, carry):
                pltpu.make_async_copy(yb.at[pl.ds(r, 1)], o_hbm.at[pl.ds(dst_ref[base + r], 1)],
                                      sem_out).start()
                return carry
            lax.fori_loop(0, n_valid, issue, 0)

            @pl.when(i == n_used - 1)
            def _():
                wait_scatter(n_valid)


def _moe_experts(plan, hn, wg, wu, wd, tm, tf):
    m, d = hn.shape
    dff = wg.shape[2]
    n_tiles = plan["tile_expert"].shape[0]
    nf = dff // tf

    def fblk(i, f, nt):
        return jnp.where(i < nt[0], f, nf - 1)

    wmap_in = lambda i, f, te, nv, nt, src, tok, dst: (te[i], 0, fblk(i, f, nt))
    wmap_out = lambda i, f, te, nv, nt, src, tok, dst: (te[i], fblk(i, f, nt), 0)
    grid_spec = pltpu.PrefetchScalarGridSpec(
        num_scalar_prefetch=6,
        grid=(n_tiles, nf),
        in_specs=[pl.BlockSpec(memory_space=pl.ANY),
                  pl.BlockSpec((1, d, tf), wmap_in), pl.BlockSpec((1, d, tf), wmap_in),
                  pl.BlockSpec((1, tf, d), wmap_out)],
        out_specs=pl.BlockSpec(memory_space=pl.ANY),
        scratch_shapes=[pltpu.VMEM((2, tm, d), F32), pltpu.VMEM((tm, d), BF16), pltpu.VMEM((tm, d), F32),
                        pltpu.VMEM((tm, d), F32), pltpu.SemaphoreType.DMA((2,)), pltpu.SemaphoreType.DMA],
    )
    return pl.pallas_call(
        functools.partial(_moe_kernel, tm=tm, nf=nf),
        grid_spec=grid_spec,
        out_shape=jax.ShapeDtypeStruct((TOP_K * m, d), F32),
        compiler_params=_cp(("arbitrary", "arbitrary")),
        name="moe_experts",
    )(plan["tile_expert"], plan["tile_valid"], plan["n_tiles"], plan["tile_src"], plan["row_token"],
      plan["row_dst"], hn, wg, wu, wd)


def _route(logits, n_experts, tm):
    m = logits.shape[0]
    top_v, top_i = lax.top_k(logits[:, :n_experts], TOP_K)
    gates = jax.nn.softmax(top_v, axis=-1)
    e_flat = top_i.reshape(-1).astype(jnp.int32)
    n_assign = m * TOP_K
    _, a_sorted = lax.sort((e_flat, jnp.arange(n_assign, dtype=jnp.int32)), num_keys=1, is_stable=True)
    counts = jnp.sum((e_flat[:, None] == jnp.arange(n_experts, dtype=jnp.int32)[None, :])
                     .astype(jnp.int32), axis=0)
    group_start = jnp.cumsum(counts) - counts
    tiles_per = (counts + tm - 1) // tm
    tile_end = jnp.cumsum(tiles_per)
    tile_start = tile_end - tiles_per
    n_tiles_max = n_assign // tm + n_experts
    tiles = jnp.arange(n_tiles_max, dtype=jnp.int32)
    used = tiles < tile_end[-1]
    tile_expert = jnp.sum((jnp.minimum(tiles, tile_end[-1] - 1)[:, None] >= tile_end[None, :])
                          .astype(jnp.int32), axis=1)
    filled = jnp.clip(counts[tile_expert] - (tiles - tile_start[tile_expert]) * tm, 0, tm)
    tile_valid = jnp.where(used, filled, 0).astype(jnp.int32)
    tile_src = group_start[tile_expert] + (tiles - tile_start[tile_expert]) * tm
    tile_src = jnp.where(used, tile_src, 0).astype(jnp.int32)
    a_pad = jnp.concatenate([a_sorted, jnp.zeros((tm,), jnp.int32)])
    token = a_pad // TOP_K
    plan = {
        "tile_expert": tile_expert.astype(jnp.int32),
        "tile_valid": tile_valid,
        "n_tiles": tile_end[-1:].astype(jnp.int32),
        "tile_src": tile_src,
        "row_token": token,
        "row_dst": (a_pad % TOP_K) * m + token,
    }
    return plan, gates


def _combine_kernel(x_ref, y0_ref, y1_ref, gate_ref, g_ref, o_ref, *, final_norm):
    gate = gate_ref[...]
    y = x_ref[...] + gate[:, 0:1] * y0_ref[...] + gate[:, 1:2] * y1_ref[...]
    o_ref[...] = _rms(y, g_ref[...]) if final_norm else y


def _combine(x2d, y2, gates, g_final, final_norm, tm):
    m, d = x2d.shape
    nblk = m // tm
    return pl.pallas_call(
        functools.partial(_combine_kernel, final_norm=final_norm),
        grid=(nblk,),
        in_specs=[pl.BlockSpec((tm, d), lambda i: (i, 0)), pl.BlockSpec((tm, d), lambda i: (i, 0)),
                  pl.BlockSpec((tm, d), lambda i: (i + nblk, 0)), pl.BlockSpec((tm, TOP_K), lambda i: (i, 0)),
                  pl.BlockSpec((1, d), lambda i: (0, 0))],
        out_specs=pl.BlockSpec((tm, d), lambda i: (i, 0)),
        out_shape=jax.ShapeDtypeStruct((m, d), F32),
        compiler_params=_cp(("parallel",)),
        name="moe_combine",
    )(x2d, y2, y2, gates, g_final)


def _tile(m, pref):
    t = min(m, pref)
    while m % t:
        t //= 2
    return t


def _block_diag_ones(width, blk):
    i = jnp.arange(width) // blk
    return (i[:, None] == i[None, :]).astype(BF16)


def _rwkv_params(l, shift_mu, w0, w_up, a0, a_up, g_up, k_k, k_a, r_k, lnx_w, lnx_b):
    width = w0.shape[1]
    lora = jnp.zeros((W_LORA + A_LORA, 2 * width), F32)
    lora = lora.at[:W_LORA, :width].set(w_up[l]).at[W_LORA:, width:].set(a_up[l])
    lora_hi = lora.astype(BF16)
    lora_lo = (lora - lora_hi.astype(F32)).astype(BF16)
    vec = jnp.zeros((8, width), F32)
    vec = vec.at[0].set(w0[l]).at[1].set(a0[l]).at[2].set(k_k[l]).at[3].set(k_a[l])
    vec = vec.at[4].set(r_k[l].reshape(-1))
    return {
        "mu": shift_mu[l][None, :], "lora_hi": lora_hi, "lora_lo": lora_lo, "vec": vec,
        "g_up": g_up[l].astype(BF16), "bd": _block_diag_ones(width, RWKV_HEAD_DIM),
        "ln": jnp.stack([lnx_w[l], lnx_b[l]]),
    }


def kernel(x_prompt, x_sample, cache_k, cache_v, state_wkv, state_shift, page_table, norm_mix, w_in, shift_mu, w0, w_up, a0, a_up, g_up, k_k, k_a, r_k, lnx_w, lnx_b, lam_q1, lam_k1, lam_q2, lam_k2, subln_w, w_out, norm_ffn, dense_w_gate, dense_w_up, dense_w_down, router_w, moe_w_gate, moe_w_up, moe_w_down, norm_final):
    depth = w_in.shape[0]
    nb_p, t_p, d = x_prompt.shape
    nb_s, t_s, _ = x_sample.shape
    width = w0.shape[1]
    rcols = shift_mu.shape[1]
    dw = (w_in.shape[2] - rcols) // 3
    nh_d = dw // DIFF_HEAD_DIM
    nh_r = width // RWKV_HEAD_DIM
    n_pool, page = cache_k.shape[1], cache_k.shape[2]
    n_pages = page_table.shape[1]
    n_experts = router_w.shape[2]
    m_p = nb_p * t_p
    m_s = nb_s * t_s

    xp = x_prompt.reshape(m_p, d)
    xs = x_sample.reshape(m_s, d)
    ck = cache_k.reshape(depth * n_pool, page * nh_d, DIFF_HEAD_DIM)
    cv = cache_v.reshape(depth * n_pool, page * nh_d, DIFF_HEAD_DIM)
    t_pad = 8
    slopes = 2.0 ** (-8.0 * jnp.arange(1, nh_d + 1, dtype=F32) / nh_d) * LOG2E
    row_slope = jnp.repeat(slopes, 2 * t_pad)[:, None]

    tm_p = _tile(m_p, 512)
    tm_s = _tile(m_s, 128)
    kv_p, kv_s = None, None
    p_state, p_shift, s_state, s_shift = [], [], [], []

    for l in range(depth):
        lam_init = 0.8 - 0.6 * math.exp(-0.3 * l)
        lam = (jnp.exp(jnp.sum(lam_q1[l] * lam_k1[l])) - jnp.exp(jnp.sum(lam_q2[l] * lam_k2[l]))
               + lam_init).reshape(1).astype(F32)
        lam = jnp.concatenate([lam, slopes])
        rw = _rwkv_params(l, shift_mu, w0, w_up, a0, a_up, g_up, k_k, k_a, r_k, lnx_w, lnx_b)
        w_in_bf = w_in[l].astype(BF16)
        w_out_bf = w_out[l].astype(BF16)
        g_mix = norm_mix[l][None, :]
        g_ffn = norm_ffn[l][None, :]
        g_fin = norm_final[None, :]
        sub_w = subln_w[l][None, :]
        last = l == depth - 1
        routed = l % 2 == 1
        router = None
        if routed:
            rt = jnp.zeros((d, LANES), F32).at[:, :n_experts].set(router_w[l // 2])
            rt_hi = rt.astype(BF16)
            router = (rt_hi, (rt - rt_hi.astype(F32)).astype(BF16))

        pr, q1, q2, k_f, v_f, k_b, v_b = _in_proj(xp, g_mix, w_in_bf, rcols, dw, tm_p, l, depth, kv_p)
        kv_p = (k_f, v_f)
        pr3 = pr.reshape(nb_p, t_p, rcols)
        z0 = jnp.zeros((nb_p, nh_r // 2, PAIR, PAIR), F32)
        o_r, z_end = _rwkv_mix(pr3, jnp.zeros((nb_p, 1, rcols), F32), z0, rw, None,
                               _tile(t_p, 256), _tile(t_p, 256))
        o_d = _diff_attn_prompt(lam, q1, q2, k_b, v_b, sub_w, nb_p, t_p, lam_init, _tile(t_p, 512))
        p_state.append(_z_to_state(z_end))
        p_shift.append(pr3[:, -1])
        xp_res = _out_proj(xp, o_r.reshape(m_p, width), o_d, w_out_bf, g_ffn, router, tm_p)

        pr_s, q1s, q2s, ks_f, vs_f, ks_b, vs_b = _in_proj(xs, g_mix, w_in_bf, rcols, dw, tm_s, l, depth, kv_s)
        kv_s = (ks_f, vs_f)
        pr_s3 = pr_s.reshape(nb_s, t_s, rcols)
        pr_pad = jnp.pad(pr_s3, ((0, 0), (0, CHUNK - t_s), (0, 0)))
        o_rs, z_end_s = _rwkv_mix(pr_pad, state_shift[l][:, None, :], _state_to_z(state_wkv[l]), rw,
                                  t_s, CHUNK, CHUNK)
        o_rs = o_rs[:, :t_s].reshape(m_s, width)
        qs = jnp.stack([q1s, q2s]).reshape(2, nb_s, t_s, nh_d, DIFF_HEAD_DIM)
        qs = jnp.pad(jnp.transpose(qs, (1, 3, 0, 2, 4)), ((0, 0),) * 3 + ((0, t_pad - t_s), (0, 0)))
        qh = qs.reshape(nb_s, nh_d * 2 * t_pad, DIFF_HEAD_DIM)

        def new_rows(z):
            z = jnp.swapaxes(z.reshape(nb_s, t_s, nh_d, DIFF_HEAD_DIM), 1, 2)
            return jnp.pad(z, ((0, 0), (0, 0), (0, page - t_s), (0, 0)))

        page_ids = (page_table.astype(jnp.int32) + l * n_pool).reshape(-1)
        o_ds = _diff_attn_decode(page_ids, lam, qh, ck, cv, new_rows(ks_b), new_rows(vs_b), sub_w,
                                 row_slope, t_s, lam_init, _tile(n_pages, 8))
        o_ds = o_ds[:, :t_s].reshape(m_s, dw)
        s_state.append(_z_to_state(z_end_s))
        s_shift.append(pr_s3[:, -1])
        xs_res = _out_proj(xs, o_rs, o_ds, w_out_bf, g_ffn, router, tm_s)

        if not routed:
            wg = dense_w_gate[l // 2].astype(BF16)
            wu = dense_w_up[l // 2].astype(BF16)
            wd = dense_w_down[l // 2].astype(BF16)
            tf = _tile(wg.shape[1], 512)
            xp = _ffn_dense(xp_res[1], xp_res[0], wg, wu, wd, g_fin, last, _tile(m_p, 1024), tf)
            xs = _ffn_dense(xs_res[1], xs_res[0], wg, wu, wd, g_fin, last, tm_s, tf)
        else:
            wg = moe_w_gate[l // 2].astype(BF16)
            wu = moe_w_up[l // 2].astype(BF16)
            wd = moe_w_down[l // 2].astype(BF16)
            tf = _tile(wg.shape[2], 512)
            outs = []
            for (xn, hn, lg), m, tm_e, tm_c in ((xp_res, m_p, _tile(m_p, 512), tm_p),
                                               (xs_res, m_s, _tile(m_s, 128), tm_s)):
                plan, gates = _route(lg, n_experts, tm_e)
                y2 = _moe_experts(plan, hn, wg, wu, wd, tm_e, tf)
                outs.append(_combine(xn, y2, gates, g_fin, last, tm_c))
            xp, xs = outs

    y_prompt = xp.reshape(nb_p, t_p, d)
    y_sample = xs.reshape(nb_s, t_s, d)
    kv5 = lambda z, nb, t: z.reshape(depth, nb, t, nh_d, DIFF_HEAD_DIM)
    return (y_prompt, y_sample, kv5(kv_p[0], nb_p, t_p), kv5(kv_p[1], nb_p, t_p),
            kv5(kv_s[0], nb_s, t_s), kv5(kv_s[1], nb_s, t_s),
            jnp.stack(p_state), jnp.stack(p_shift), jnp.stack(s_state), jnp.stack(s_shift))
```
